```python
import jax, jax.numpy as jnp
from jax import lax
import numpy as np

D_MODEL = 4096
BATCH = 4
SEQ = 2048
DEPTH = 2
DEC_BATCH = 2
DEC_SEQ = 4096
PAST_LEN = 128

GRID_W = 64
BLOCK_Q = 128
ATT_HEAD_DIM = 128
ATT_WIDTH = D_MODEL // 2
ATT_Q_HEADS = ATT_WIDTH // ATT_HEAD_DIM
ATT_KV_HEADS = ATT_Q_HEADS // 4
ATT_GROUPS = ATT_Q_HEADS // ATT_KV_HEADS
ATT_KV_WIDTH = ATT_KV_HEADS * ATT_HEAD_DIM
ROPE_THETA = 10000.0
RWKV_HEAD_DIM = 64
RWKV_WIDTH = D_MODEL // 2
RWKV_HEADS = RWKV_WIDTH // RWKV_HEAD_DIM
N_DIR = 2
DECAY_LORA = 128
ICLR_LORA = 128
GATE_LORA = 480
D_FF = 256 * (-(-(8 * D_MODEL) // (3 * 256)))
NORM_EPS = 1e-6
LNX_EPS = RWKV_HEAD_DIM * 1e-5
IN_SPLITS = (ATT_WIDTH, ATT_KV_WIDTH, ATT_KV_WIDTH, 3 * RWKV_WIDTH, N_DIR * DECAY_LORA,
             N_DIR * ICLR_LORA, GATE_LORA, D_MODEL, D_MODEL)
IN_WIDTH = sum(IN_SPLITS)

kernel_name = "hybrid_gqa_axialrope_rwkv7_bidir_convffn_encoder"


def rms_norm(x, g, eps=NORM_EPS):
    xf = x.astype(jnp.float32)
    y = xf * lax.rsqrt(jnp.mean(xf * xf, axis=-1, keepdims=True) + eps)
    return (y * g.astype(jnp.float32)).astype(x.dtype)


def dwconv3(x, w, b=None):
    xp = jnp.pad(x, ((0, 0), (1, 1), (0, 0)))
    y = xp[:, :-2] * w[0] + xp[:, 1:-1] * w[1] + xp[:, 2:] * w[2]
    return y if b is None else y + b


def axial_rope_tables(T):
    rows = T // GRID_W
    row = jnp.repeat(jnp.arange(rows, dtype=jnp.float32), GRID_W)
    col = jnp.tile(jnp.arange(GRID_W, dtype=jnp.float32), rows)
    axis_dim = ATT_HEAD_DIM // 2
    inv = ROPE_THETA ** (-jnp.arange(0, axis_dim, 2, dtype=jnp.float32) / axis_dim)
    ang_r = row[:, None] * inv
    ang_c = col[:, None] * inv
    return (jnp.cos(ang_r), jnp.sin(ang_r), jnp.cos(ang_c), jnp.sin(ang_c))


def rotate_axis(x, cos, sin):
    x1, x2 = jnp.split(x, 2, axis=-1)
    c = cos[:, None, :].astype(x.dtype)
    s = sin[:, None, :].astype(x.dtype)
    return jnp.concatenate([x1 * c - x2 * s, x1 * s + x2 * c], axis=-1)


def apply_axial_rope(x, rope):
    cos_r, sin_r, cos_c, sin_c = rope
    xr, xc = jnp.split(x, 2, axis=-1)
    return jnp.concatenate([rotate_axis(xr, cos_r, sin_r), rotate_axis(xc, cos_c, sin_c)], axis=-1)


def block_attention(q, k, v):
    B, T = q.shape[:2]
    nb = T // BLOCK_Q
    qb = q.reshape(B, nb, BLOCK_Q, ATT_KV_HEADS, ATT_GROUPS, ATT_HEAD_DIM).transpose(1, 0, 2, 3, 4, 5)
    scale = ATT_HEAD_DIM ** -0.5

    def one_block(qi):
        s = jnp.einsum('bqhgd,bkhd->bhgqk', qi, k).astype(jnp.float32) * scale
        p = jax.nn.softmax(s, axis=-1).astype(v.dtype)
        return jnp.einsum('bhgqk,bkhd->bqhgd', p, v)

    o = lax.map(one_block, qb)
    return o.transpose(1, 0, 2, 3, 4, 5).reshape(B, T, ATT_WIDTH)


def _both_dirs(x):
    return jnp.stack([x, jnp.flip(x, axis=1)], axis=0)


def _flip_second(x):
    return jnp.stack([x[0], jnp.flip(x[1], axis=1)], axis=0)


def rwkv7_scan(r, w, k, v, kk, b):
    def step(S, inp):
        r_t, w_t, k_t, v_t, kk_t, b_t = inp
        sa = jnp.einsum('dbhvk,dbhk->dbhv', S, kk_t)
        S = S * w_t[..., None, :] - sa[..., None] * b_t[..., None, :] + v_t[..., :, None] * k_t[..., None, :]
        return S, jnp.einsum('dbhvk,dbhk->dbhv', S, r_t)

    xs = tuple(jnp.moveaxis(a, 2, 0) for a in (r, w, k, v, kk, b))
    S0 = jnp.zeros(r.shape[:2] + (RWKV_HEADS, RWKV_HEAD_DIM, RWKV_HEAD_DIM), jnp.float32)
    _, ys = lax.scan(step, S0, xs)
    return jnp.moveaxis(ys, 0, 2)


def rwkv7_bidirectional(rkv, w_low, a_low, g_low, decay_w0, decay_w2, iclr_a0, iclr_a2, gate_g2,
                        k_k, k_a, r_k, lnx_w, lnx_b):
    B, T, _ = rkv.shape
    dt = rkv.dtype
    H, K = RWKV_HEADS, RWKV_HEAD_DIM
    f32 = jnp.float32
    heads = lambda z: z.reshape(z.shape[:-1] + (H, K))
    r, k, v = jnp.split(rkv.astype(f32), 3, axis=-1)
    wl = jnp.tanh(w_low.astype(f32).reshape(B, T, N_DIR, DECAY_LORA))
    w_raw = decay_w0.astype(f32)[:, None, None, :] + jnp.einsum('btdr,drc->dbtc', wl, decay_w2.astype(f32))
    decay = jnp.exp(-jnp.exp(-jax.nn.softplus(-w_raw) - 0.5))
    al = a_low.astype(f32).reshape(B, T, N_DIR, ICLR_LORA)
    a_gate = jax.nn.sigmoid(iclr_a0.astype(f32)[:, None, None, :]
                            + jnp.einsum('btdr,drc->dbtc', al, iclr_a2.astype(f32)))
    g = jax.nn.sigmoid(g_low.astype(f32)) @ gate_g2.astype(f32)
    kk = heads(k * k_k.astype(f32))
    kk = kk / jnp.maximum(jnp.sqrt(jnp.sum(kk * kk, axis=-1, keepdims=True)), 1e-12)
    k_dir = heads(k[None] * (1.0 + (a_gate - 1.0) * k_a.astype(f32)))
    b = heads(a_gate) * kk[None]
    rh, vh = heads(r), heads(v)
    ys = rwkv7_scan(_both_dirs(rh), _flip_second(heads(decay)), _flip_second(k_dir),
                    _both_dirs(vh), _both_dirs(kk), _flip_second(b))
    y = ys[0] + jnp.flip(ys[1], axis=1)
    mu = jnp.mean(y, axis=-1, keepdims=True)
    var = jnp.mean(jnp.square(y - mu), axis=-1, keepdims=True)
    y = (y - mu) * lax.rsqrt(var + LNX_EPS) * heads(lnx_w.astype(f32)) + heads(lnx_b.astype(f32))
    bonus = jnp.sum(rh[None] * k_dir * r_k.astype(f32), axis=-1, keepdims=True) * vh[None]
    y = y + jnp.sum(bonus, axis=0)
    return (y.reshape(B, T, H * K) * g).astype(dt)


def encoder_layer(x, rope, norm_mix, w_in, q_gain, k_gain, rwkv_conv, decay_w0, decay_w2, iclr_a0,
                  iclr_a2, gate_g2, k_k, k_a, r_k, lnx_w, lnx_b, w_up_attn, w_up_rwkv, w_o,
                  norm_ffn, w_ffn_up, ffn_conv, ffn_conv_b, w_ffn_down):
    B, T, _ = x.shape
    h = rms_norm(x, norm_mix)
    proj = h @ w_in
    points = np.cumsum(IN_SPLITS)[:-1].tolist()
    q, k, v, rkv, w_low, a_low, g_low, gate_att, gate_rwkv = jnp.split(proj, points, axis=-1)
    q = apply_axial_rope(rms_norm(q.reshape(B, T, ATT_Q_HEADS, ATT_HEAD_DIM), q_gain), rope)
    k = apply_axial_rope(rms_norm(k.reshape(B, T, ATT_KV_HEADS, ATT_HEAD_DIM), k_gain), rope)
    v = v.reshape(B, T, ATT_KV_HEADS, ATT_HEAD_DIM)
    att = block_attention(q, k, v)
    rkv = dwconv3(rkv, rwkv_conv)
    rw = rwkv7_bidirectional(rkv, w_low, a_low, g_low, decay_w0, decay_w2, iclr_a0, iclr_a2,
                             gate_g2, k_k, k_a, r_k, lnx_w, lnx_b)
    mixed = jax.nn.sigmoid(gate_att) * (att @ w_up_attn) + jax.nn.sigmoid(gate_rwkv) * (rw @ w_up_rwkv)
    x = x + mixed @ w_o
    h = rms_norm(x, norm_ffn)
    u = dwconv3(h @ w_ffn_up, ffn_conv, ffn_conv_b)
    val, gate = jnp.split(u, 2, axis=-1)
    return x + (jax.nn.silu(gate) * val) @ w_ffn_down


def run_trunk(x, weights):
    rope = axial_rope_tables(x.shape[1])
    for l in range(DEPTH):
        x = encoder_layer(x, rope, *[w[l] for w in weights])
    return x


def setup_inputs(seed: int = 0) -> dict:
    key = jax.random.key(seed)
    ks = jax.random.split(key, 25)
    L, D = DEPTH, D_MODEL
    f32 = jnp.float32

    def nrm(i, shape, scale):
        return scale * jax.random.normal(ks[i], shape, f32)

    x_prompt = nrm(0, (BATCH, SEQ, D), 1.0)
    x_sample = nrm(1, (DEC_BATCH, DEC_SEQ, D), 1.0)
    norm_mix = 1.0 + nrm(2, (L, D), 0.02)
    w_in = nrm(3, (L, D, IN_WIDTH), D ** -0.5)
    q_gain = 1.0 + nrm(4, (L, ATT_HEAD_DIM), 0.02)
    k_gain = 1.0 + nrm(5, (L, ATT_HEAD_DIM), 0.02)
    rwkv_conv = jnp.array([0.25, 0.5, 0.25], f32)[None, :, None] + nrm(6, (L, 3, 3 * RWKV_WIDTH), 0.1)
    decay_w0 = jax.random.uniform(ks[7], (L, N_DIR, RWKV_WIDTH), f32, -6.0, -1.0)
    decay_w2 = nrm(8, (L, N_DIR, DECAY_LORA, RWKV_WIDTH), 0.1 * DECAY_LORA ** -0.5)
    iclr_a0 = nrm(9, (L, N_DIR, RWKV_WIDTH), 0.1)
    iclr_a2 = nrm(10, (L, N_DIR, ICLR_LORA, RWKV_WIDTH), 0.3 * ICLR_LORA ** -0.5)
    gate_g2 = nrm(11, (L, GATE_LORA, RWKV_WIDTH), GATE_LORA ** -0.5)
    k_k = 0.85 + nrm(12, (L, RWKV_WIDTH), 0.05)
    k_a = 1.0 + nrm(13, (L, RWKV_WIDTH), 0.05)
    r_k = nrm(14, (L, RWKV_HEADS, RWKV_HEAD_DIM), 0.1)
    lnx_w = 1.0 + nrm(15, (L, RWKV_WIDTH), 0.02)
    lnx_b = nrm(16, (L, RWKV_WIDTH), 0.02)
    w_up_attn = nrm(17, (L, ATT_WIDTH, D), ATT_WIDTH ** -0.5)
    w_up_rwkv = nrm(18, (L, RWKV_WIDTH, D), RWKV_WIDTH ** -0.5)
    w_o = nrm(19, (L, D, D), D ** -0.5)
    norm_ffn = 1.0 + nrm(20, (L, D), 0.02)
    w_ffn_up = nrm(21, (L, D, 2 * D_FF), D ** -0.5)
    ffn_conv = nrm(22, (L, 3, 2 * D_FF), 3 ** -0.5)
    ffn_conv_b = nrm(23, (L, 2 * D_FF), 0.02)
    w_ffn_down = nrm(24, (L, D_FF, D), D_FF ** -0.5)
    return {"x_prompt": x_prompt, "x_sample": x_sample, "norm_mix": norm_mix, "w_in": w_in,
            "q_gain": q_gain, "k_gain": k_gain, "rwkv_conv": rwkv_conv, "decay_w0": decay_w0,
            "decay_w2": decay_w2, "iclr_a0": iclr_a0, "iclr_a2": iclr_a2, "gate_g2": gate_g2,
            "k_k": k_k, "k_a": k_a, "r_k": r_k, "lnx_w": lnx_w, "lnx_b": lnx_b,
            "w_up_attn": w_up_attn, "w_up_rwkv": w_up_rwkv, "w_o": w_o, "norm_ffn": norm_ffn,
            "w_ffn_up": w_ffn_up, "ffn_conv": ffn_conv, "ffn_conv_b": ffn_conv_b,
            "w_ffn_down": w_ffn_down}


def reference(x_prompt, x_sample, norm_mix, w_in, q_gain, k_gain, rwkv_conv, decay_w0, decay_w2,
              iclr_a0, iclr_a2, gate_g2, k_k, k_a, r_k, lnx_w, lnx_b, w_up_attn, w_up_rwkv, w_o,
              norm_ffn, w_ffn_up, ffn_conv, ffn_conv_b, w_ffn_down):
    weights = (norm_mix, w_in, q_gain, k_gain, rwkv_conv, decay_w0, decay_w2, iclr_a0, iclr_a2,
               gate_g2, k_k, k_a, r_k, lnx_w, lnx_b, w_up_attn, w_up_rwkv, w_o, norm_ffn,
               w_ffn_up, ffn_conv, ffn_conv_b, w_ffn_down)
    y_prompt = run_trunk(x_prompt, weights)
    y_sample = run_trunk(x_sample, weights)
    return (y_prompt, y_sample)
```

```python
import functools

import jax
import jax.numpy as jnp
import numpy as np
from jax import lax
from jax.experimental import pallas as pl
from jax.experimental.pallas import tpu as pltpu

F32 = jnp.float32
BF16 = jnp.bfloat16
HI = lax.Precision.HIGHEST

D_MODEL = 4096
P_BATCH, P_SEQ = 4, 2048
S_BATCH, S_SEQ = 2, 4096
N_PROMPT = P_BATCH * P_SEQ
N_TOK = N_PROMPT + S_BATCH * S_SEQ
GRID_W = 64
HEAD = 128
ATT_W = 2048
KV_W = 512
KV_HEADS = 4
GROUPS = 4
ROPE_THETA = 10000.0
RK = 64
RW = 2048
LORA = 128
GATE_LORA = 480
GATE_LORA_PAD = 512
D_FF = 11008
D_FF_PAD = 11264
NORM_EPS = 1e-6
LNX_EPS = RK * 1e-5

C_Q, C_K, C_V = 0, 2048, 2560
C_R, C_RK, C_RV = 3072, 5120, 7168
C_WL, C_AL, C_GL = 9216, 9472, 9728
C_GA, C_GR = 10240, 14336
IN_PAD = 18432

CHUNK = 64
VMEM_LIMIT = 56 * 1024 * 1024


def _cparams(sem):
    return pltpu.CompilerParams(dimension_semantics=sem, vmem_limit_bytes=VMEM_LIMIT)


def _seq_edges(rows):
    pos = rows & jnp.where(rows < N_PROMPT, P_SEQ - 1, S_SEQ - 1)
    last = jnp.where(rows < N_PROMPT, P_SEQ - 1, S_SEQ - 1)
    return pos == 0, pos == last


def _neighbours(x, prev_blk, next_blk, row0):
    tm = x.shape[0]
    li = lax.broadcasted_iota(jnp.int32, x.shape, 0)
    start, end = _seq_edges(row0 + li)
    xp = jnp.where(li == 0, prev_blk[7:8, :], pltpu.roll(x, 1, axis=0))
    xn = jnp.where(li == tm - 1, next_blk[0:1, :], pltpu.roll(x, tm - 1, axis=0))
    return jnp.where(start, 0.0, xp), jnp.where(end, 0.0, xn)


def _rmsnorm_kernel(x_ref, g_ref, o_ref):
    x = x_ref[...]
    ms = jnp.mean(x * x, axis=-1, keepdims=True)
    o_ref[...] = (x * lax.rsqrt(ms + NORM_EPS) * g_ref[...]).astype(o_ref.dtype)


def rmsnorm_cast(x, g, tm=256):
    n, d = x.shape
    return pl.pallas_call(
        _rmsnorm_kernel,
        grid=(n // tm,),
        in_specs=[pl.BlockSpec((tm, d), lambda i: (i, 0)), pl.BlockSpec((1, d), lambda i: (0, 0))],
        out_specs=pl.BlockSpec((tm, d), lambda i: (i, 0)),
        out_shape=jax.ShapeDtypeStruct((n, d), BF16),
        compiler_params=_cparams(("parallel",)),
        name="rmsnorm_cast",
    )(x, g.reshape(1, d))


def _mm_kernel(a_ref, b_ref, o_ref):
    o_ref[...] = jnp.dot(a_ref[...], b_ref[...], preferred_element_type=F32).astype(o_ref.dtype)


def _mm_res_kernel(a_ref, b_ref, r_ref, o_ref):
    o_ref[...] = r_ref[...] + jnp.dot(a_ref[...], b_ref[...], preferred_element_type=F32)


def matmul(a, b, res=None, *, tm=512, tn=512, out_dtype=F32, name="matmul"):
    m, k = a.shape
    n = b.shape[1]
    in_specs = [pl.BlockSpec((tm, k), lambda j, i: (i, 0)), pl.BlockSpec((k, tn), lambda j, i: (0, j))]
    args = [a, b]
    kern = _mm_kernel
    if res is not None:
        in_specs.append(pl.BlockSpec((tm, tn), lambda j, i: (i, j)))
        args.append(res)
        kern = _mm_res_kernel
    return pl.pallas_call(
        kern,
        grid=(n // tn, m // tm),
        in_specs=in_specs,
        out_specs=pl.BlockSpec((tm, tn), lambda j, i: (i, j)),
        out_shape=jax.ShapeDtypeStruct((m, n), out_dtype),
        compiler_params=_cparams(("parallel", "parallel")),
        name=name,
    )(*args)


def _mmk_res_kernel(a_ref, b_ref, r_ref, o_ref, acc_ref):
    kk = pl.program_id(2)

    @pl.when(kk == 0)
    def _():
        acc_ref[...] = r_ref[...]

    acc_ref[...] += jnp.dot(a_ref[...], b_ref[...], preferred_element_type=F32)

    @pl.when(kk == pl.num_programs(2) - 1)
    def _():
        o_ref[...] = acc_ref[...]


def matmul_ktiled_res(a, b, res, *, tm=1024, tn=1024, tk=1408, name="matmul_k"):
    m, k = a.shape
    n = b.shape[1]
    return pl.pallas_call(
        _mmk_res_kernel,
        grid=(m // tm, n // tn, k // tk),
        in_specs=[
            pl.BlockSpec((tm, tk), lambda i, j, kk: (i, kk)),
            pl.BlockSpec((tk, tn), lambda i, j, kk: (kk, j)),
            pl.BlockSpec((tm, tn), lambda i, j, kk: (i, j)),
        ],
        out_specs=pl.BlockSpec((tm, tn), lambda i, j, kk: (i, j)),
        out_shape=jax.ShapeDtypeStruct((m, n), F32),
        scratch_shapes=[pltpu.VMEM((tm, tn), F32)],
        compiler_params=_cparams(("parallel", "parallel", "arbitrary")),
        name=name,
    )(a, b, res)


def _merge_kernel(att_ref, rw_ref, wa_ref, wr_ref, ga_ref, gr_ref, o_ref):
    ya = jnp.dot(att_ref[...], wa_ref[...], preferred_element_type=F32)
    yr = jnp.dot(rw_ref[...], wr_ref[...], preferred_element_type=F32)
    o_ref[...] = (jax.nn.sigmoid(ga_ref[...]) * ya + jax.nn.sigmoid(gr_ref[...]) * yr).astype(o_ref.dtype)


def merge_branches(att, rw, wa, wr, proj, tm=512, tn=512):
    n = att.shape[0]
    ga0, gr0 = C_GA // tn, C_GR // tn
    return pl.pallas_call(
        _merge_kernel,
        grid=(D_MODEL // tn, n // tm),
        in_specs=[
            pl.BlockSpec((tm, ATT_W), lambda j, i: (i, 0)),
            pl.BlockSpec((tm, RW), lambda j, i: (i, 0)),
            pl.BlockSpec((ATT_W, tn), lambda j, i: (0, j)),
            pl.BlockSpec((RW, tn), lambda j, i: (0, j)),
            pl.BlockSpec((tm, tn), lambda j, i: (i, ga0 + j)),
            pl.BlockSpec((tm, tn), lambda j, i: (i, gr0 + j)),
        ],
        out_specs=pl.BlockSpec((tm, tn), lambda j, i: (i, j)),
        out_shape=jax.ShapeDtypeStruct((n, D_MODEL), BF16),
        compiler_params=_cparams(("parallel", "parallel")),
        name="merge_branches",
    )(att, rw, wa, wr, proj, proj)


def _qkv_prep_kernel(x_ref, gain_ref, cos_ref, sa_ref, sb_ref, o_ref):
    j = pl.program_id(1)

    @pl.when(j < 5)
    def _():
        cos, sa, sb = cos_ref[...], sa_ref[...], sb_ref[...]
        gain = gain_ref[0]
        for h in range(4):
            x = x_ref[:, h * HEAD:(h + 1) * HEAD]
            ms = jnp.mean(x * x, axis=-1, keepdims=True)
            y = x * lax.rsqrt(ms + NORM_EPS) * gain
            y = y * cos + pltpu.roll(y, 96, axis=1) * sa + pltpu.roll(y, 32, axis=1) * sb
            o_ref[:, h * HEAD:(h + 1) * HEAD] = y.astype(o_ref.dtype)

    @pl.when(j == 5)
    def _():
        o_ref[...] = x_ref[...].astype(o_ref.dtype)


def _rope_tables():
    t = np.arange(S_SEQ)
    row = (t // GRID_W).astype(np.float32)
    col = (t % GRID_W).astype(np.float32)
    axis_dim = HEAD // 2
    inv = jnp.asarray(ROPE_THETA, F32) ** (-jnp.arange(0, axis_dim, 2, dtype=F32) / axis_dim)
    ang_r = jnp.asarray(row)[:, None] * inv
    ang_c = jnp.asarray(col)[:, None] * inv
    cr, sr, cc, sc = jnp.cos(ang_r), jnp.sin(ang_r), jnp.cos(ang_c), jnp.sin(ang_c)
    z = jnp.zeros_like(sr)
    cos = jnp.concatenate([cr, cr, cc, cc], axis=1)
    sin_a = jnp.concatenate([-sr, z, -sc, z], axis=1)
    sin_b = jnp.concatenate([z, sr, z, sc], axis=1)
    return cos, sin_a, sin_b


def qkv_prep(proj, gains, tables, tm=256):
    n = proj.shape[0]
    npb = N_PROMPT // tm

    def tab_map(i, j):
        return (jnp.where(i < npb, i % (P_SEQ // tm), (i - npb) % (S_SEQ // tm)), 0)

    tab_spec = pl.BlockSpec((tm, HEAD), tab_map)
    return pl.pallas_call(
        _qkv_prep_kernel,
        grid=(n // tm, 6),
        in_specs=[
            pl.BlockSpec((tm, 512), lambda i, j: (i, j)),
            pl.BlockSpec((1, 1, HEAD), lambda i, j: (j, 0, 0)),
            tab_spec, tab_spec, tab_spec,
        ],
        out_specs=pl.BlockSpec((tm, 512), lambda i, j: (i, j)),
        out_shape=jax.ShapeDtypeStruct((n, 3072), BF16),
        compiler_params=_cparams(("parallel", "parallel")),
        name="qkv_prep",
    )(proj, gains, *tables)


def _attn_kernel(q_ref, k_ref, v_ref, o_ref):
    k = k_ref[...]
    v = v_ref[...]
    scale = HEAD ** -0.5
    for g in range(GROUPS):
        q = q_ref[:, g * HEAD:(g + 1) * HEAD]
        s = lax.dot_general(q, k, (((1,), (1,)), ((), ())), preferred_element_type=F32) * scale
        m = jnp.max(s, axis=-1, keepdims=True)
        p = jnp.exp(s - m)
        l = jnp.sum(p, axis=-1, keepdims=True)
        o = jnp.dot(p.astype(BF16), v, preferred_element_type=F32)
        o_ref[:, g * HEAD:(g + 1) * HEAD] = (o / l).astype(o_ref.dtype)


def attention(qkv, row0, batch, seq, tq=256):
    qb0 = row0 // tq
    sb0 = row0 // seq
    nq = seq // tq
    return pl.pallas_call(
        _attn_kernel,
        grid=(batch, KV_HEADS, nq),
        in_specs=[
            pl.BlockSpec((tq, GROUPS * HEAD), lambda b, h, i: (qb0 + b * nq + i, h)),
            pl.BlockSpec((seq, HEAD), lambda b, h, i: (sb0 + b, C_K // HEAD + h)),
            pl.BlockSpec((seq, HEAD), lambda b, h, i: (sb0 + b, C_V // HEAD + h)),
        ],
        out_specs=pl.BlockSpec((tq, GROUPS * HEAD), lambda b, h, i: (b * nq + i, h)),
        out_shape=jax.ShapeDtypeStruct((batch * seq, ATT_W), BF16),
        compiler_params=_cparams(("parallel", "parallel", "arbitrary")),
        name="attention",
    )(qkv, qkv, qkv)


def _softplus(z):
    return jnp.maximum(z, 0.0) + jnp.log1p(jnp.exp(-jnp.abs(z)))


def _rwkv_prep_kernel(xr, xr_p, xr_n, xk, xk_p, xk_n, xv, xv_p, xv_n, wl_ref, al_ref, gl_ref,
                      cr, ck, cv, w0, w2, a0, a2, g2, kkw, kaw, rkw, bd_ref,
                      r_o, v_o, kk_o, bonus_o, g_o, lw_o, kd_o, b_o, *, tm):
    row0 = pl.program_id(0) * tm

    def conv(x_ref, p_ref, n_ref, c_ref):
        x = x_ref[...]
        xp, xn = _neighbours(x, p_ref[...], n_ref[...], row0)
        return xp * c_ref[0:1, :] + x * c_ref[1:2, :] + xn * c_ref[2:3, :]

    r = conv(xr, xr_p, xr_n, cr)
    k = conv(xk, xk_p, xk_n, ck)
    v = conv(xv, xv_p, xv_n, cv)
    bd = bd_ref[...]

    def segsum(z):
        return jnp.dot(z, bd, precision=HI, preferred_element_type=F32)

    kkr = k * kkw[...]
    kk = kkr / jnp.maximum(jnp.sqrt(segsum(kkr * kkr)), 1e-12)
    g = jnp.dot(jax.nn.sigmoid(gl_ref[...]).astype(BF16), g2[...], preferred_element_type=F32)
    r_o[...] = r
    v_o[...] = v
    kk_o[...] = kk
    g_o[...] = g
    bc = jnp.zeros_like(r)
    for d in range(2):
        wl = jnp.tanh(wl_ref[:, d * LORA:(d + 1) * LORA]).astype(BF16)
        w_raw = w0[d:d + 1, :] + jnp.dot(wl, w2[d], preferred_element_type=F32)
        lw_o[d] = -jnp.exp(-_softplus(-w_raw) - 0.5)
        al = al_ref[:, d * LORA:(d + 1) * LORA].astype(BF16)
        a = jax.nn.sigmoid(a0[d:d + 1, :] + jnp.dot(al, a2[d], preferred_element_type=F32))
        kd = k * (1.0 + (a - 1.0) * kaw[...])
        kd_o[d] = kd
        b_o[d] = a * kk
        bc = bc + r * kd * rkw[...]
    bonus_o[...] = segsum(bc) * v


def _head_block_ones(width):
    i = np.arange(width) // RK
    return jnp.asarray((i[:, None] == i[None, :]).astype(np.float32))


def rwkv_prep(proj, p, tm=256, tc=512):
    n = proj.shape[0]
    nrb = n // 8
    tb = tm // 8

    def main(c0):
        return pl.BlockSpec((tm, tc), lambda i, j: (i, c0 // tc + j))

    def prev(c0):
        return pl.BlockSpec((8, tc), lambda i, j: (jnp.maximum(i * tb - 1, 0), c0 // tc + j))

    def nxt(c0):
        return pl.BlockSpec((8, tc), lambda i, j: (jnp.minimum((i + 1) * tb, nrb - 1), c0 // tc + j))

    def chan(rows, c0=0):
        return pl.BlockSpec((rows, tc), lambda i, j: (0, c0 // tc + j))

    in_specs = []
    for c0 in (C_R, C_RK, C_RV):
        in_specs += [main(c0), prev(c0), nxt(c0)]
    in_specs += [
        pl.BlockSpec((tm, 2 * LORA), lambda i, j: (i, C_WL // (2 * LORA))),
        pl.BlockSpec((tm, 2 * LORA), lambda i, j: (i, C_AL // (2 * LORA))),
        pl.BlockSpec((tm, GATE_LORA_PAD), lambda i, j: (i, C_GL // GATE_LORA_PAD)),
        chan(3, 0), chan(3, RW), chan(3, 2 * RW),
        chan(2),
        pl.BlockSpec((2, LORA, tc), lambda i, j: (0, 0, j)),
        chan(2),
        pl.BlockSpec((2, LORA, tc), lambda i, j: (0, 0, j)),
        pl.BlockSpec((GATE_LORA_PAD, tc), lambda i, j: (0, j)),
        chan(1), chan(1), chan(1),
        pl.BlockSpec((tc, tc), lambda i, j: (0, 0)),
    ]
    one = pl.BlockSpec((tm, tc), lambda i, j: (i, j))
    two = pl.BlockSpec((2, tm, tc), lambda i, j: (0, i, j))
    s1 = jax.ShapeDtypeStruct((n, RW), F32)
    s2 = jax.ShapeDtypeStruct((2, n, RW), F32)
    return pl.pallas_call(
        functools.partial(_rwkv_prep_kernel, tm=tm),
        grid=(n // tm, RW // tc),
        in_specs=in_specs,
        out_specs=[one, one, one, one, one, two, two, two],
        out_shape=[s1, s1, s1, s1, s1, s2, s2, s2],
        compiler_params=_cparams(("parallel", "parallel")),
        name="rwkv_prep",
    )(proj, proj, proj, proj, proj, proj, proj, proj, proj, proj, proj, proj,
      p["rwkv_conv"], p["rwkv_conv"], p["rwkv_conv"], p["decay_w0"], p["decay_w2"], p["iclr_a0"],
      p["iclr_a2"], p["gate_g2"], p["k_k"], p["k_a"], p["r_k"], _head_block_ones(tc))


def _scan_kernel(r_ref, v_ref, kk_ref, lw_ref, kd_ref, b_ref, y_ref, s_ref, *, pg, nc):
    d = pl.program_id(0)
    c = pl.program_id(2)
    chunk = c + d * (nc - 1 - 2 * c)
    npc = N_PROMPT // CHUNK
    in_p = chunk < npc
    per = jnp.where(in_p, P_SEQ // CHUNK, S_SEQ // CHUNK)
    rel = jnp.where(in_p, chunk, chunk - npc)
    @pl.when((rel + d) % per == 0)
    def _():
        s_ref[...] = jnp.zeros_like(s_ref)

    sgn = 1 - 2 * d
    C = CHUNK
    t_i = lax.broadcasted_iota(jnp.int32, (C, 2 * C), 0)
    s_i = lax.broadcasted_iota(jnp.int32, (C, 2 * C), 1) & (C - 1)
    rel_ts = (t_i - s_i) * sgn
    strict = rel_ts > 0
    incl = rel_ts >= 0
    lane = lax.broadcasted_iota(jnp.int32, (C, 2 * C), 1)
    m_a = lane < C
    tt = lax.broadcasted_iota(jnp.int32, (C, C), 0)
    ss = lax.broadcasted_iota(jnp.int32, (C, C), 1)
    tri = jnp.where((tt - ss) * sgn >= 0, 1.0, 0.0).astype(F32)
    ri = lax.broadcasted_iota(jnp.int32, (2 * C, 2 * C), 0)
    ci = lax.broadcasted_iota(jnp.int32, (2 * C, 2 * C), 1)
    same_head = (ri >> 6) == (ci >> 6)
    eye = jnp.where(ri == ci, 1.0, 0.0).astype(F32)
    same8 = (ri >> 3) == (ci >> 3)
    lvl = [((ri >> (s + 1)) == (ci >> (s + 1))) & ((ri >> s) != (ci >> s)) for s in (3, 4, 5)]

    def split(x):
        return jnp.concatenate([jnp.where(m_a, x, 0.0), jnp.where(m_a, 0.0, x)], axis=0)

    def mm(a, b):
        return jnp.dot(a, b, precision=HI, preferred_element_type=F32)

    def mm_nt(a, b):
        return lax.dot_general(a, b, (((1,), (1,)), ((), ())), precision=HI, preferred_element_type=F32)

    def mm_tn(a, b):
        return lax.dot_general(a, b, (((0,), (0,)), ((), ())), precision=HI, preferred_element_type=F32)

    for p in range(pg):
        sl = slice(p * 2 * C, (p + 1) * 2 * C)
        r, v, kk = r_ref[:, sl], v_ref[:, sl], kk_ref[:, sl]
        lw, kd, b = lw_ref[0, :, sl], kd_ref[0, :, sl], b_ref[0, :, sl]
        cl = mm(tri, lw)
        tot = jnp.sum(lw, axis=0, keepdims=True)
        e_neg = jnp.exp(-cl)
        e_end = jnp.exp(tot - cl)
        x = jnp.concatenate([r * jnp.exp(cl), kk * jnp.exp(cl - lw)], axis=0)
        kt, bt = kd * e_neg, b * e_neg
        z = jnp.concatenate([split(kt), split(bt)], axis=0)
        pm = mm_nt(x, z)
        a_rk, a_kk = pm[0:C, 0:2 * C], pm[C:2 * C, 0:2 * C]
        a_rb, a_kb = pm[0:C, 2 * C:4 * C], pm[C:2 * C, 2 * C:4 * C]
        ld = split(jnp.where(strict, a_kb, 0.0))
        l8 = jnp.where(same8, ld, 0.0)
        l8_2 = mm(l8, l8)
        l8_4 = mm(l8_2, l8_2)
        inv = mm(mm(eye - l8, eye + l8_2), eye + l8_4)
        for msk in lvl:
            lo = jnp.where(msk & same_head, ld, 0.0)
            inv = inv - mm(mm(inv, lo), inv)
        st = s_ref[p]
        xs = mm_nt(x, st)
        vs = split(v)
        rhs = xs[C:2 * C] + mm(jnp.where(strict, a_kk, 0.0), vs)
        ud = mm(inv, split(rhs))
        u = ud[0:C] + ud[C:2 * C]
        coef = jnp.concatenate([jnp.where(incl, a_rk, 0.0), jnp.where(incl, -a_rb, 0.0)], axis=1)
        y_ref[0, :, sl] = xs[0:C] + mm(coef, jnp.concatenate([vs, ud], axis=0))
        upd = mm_tn(jnp.concatenate([v, u], axis=0), jnp.concatenate([kd * e_end, -(b * e_end)], axis=0))
        s_ref[p] = st * jnp.exp(tot) + jnp.where(same_head, upd, 0.0)


def rwkv_scan(r, v, kk, lw, kd, b, pg=4):
    n = r.shape[0]
    nc = n // CHUNK
    w = pg * 2 * CHUNK

    def cmap(d, g, c):
        return (c + d * (nc - 1 - 2 * c), g)

    def dmap(d, g, c):
        return (d, c + d * (nc - 1 - 2 * c), g)

    one = pl.BlockSpec((CHUNK, w), cmap)
    two = pl.BlockSpec((1, CHUNK, w), dmap)
    return pl.pallas_call(
        functools.partial(_scan_kernel, pg=pg, nc=nc),
        grid=(2, RW // w, nc),
        in_specs=[one, one, one, two, two, two],
        out_specs=two,
        out_shape=jax.ShapeDtypeStruct((2, n, RW), F32),
        scratch_shapes=[pltpu.VMEM((pg, 2 * CHUNK, 2 * CHUNK), F32)],
        compiler_params=_cparams(("parallel", "parallel", "arbitrary")),
        name="rwkv_scan",
    )(r, v, kk, lw, kd, b)


def _rwkv_post_kernel(y_ref, bonus_ref, g_ref, w_ref, b_ref, bd_ref, o_ref):
    bd = bd_ref[...]

    def segmean(z):
        return jnp.dot(z, bd, precision=HI, preferred_element_type=F32) * (1.0 / RK)

    y = y_ref[0] + y_ref[1]
    yc = y - segmean(y)
    var = segmean(yc * yc)
    out = yc * lax.rsqrt(var + LNX_EPS) * w_ref[...] + b_ref[...]
    o_ref[...] = ((out + bonus_ref[...]) * g_ref[...]).astype(o_ref.dtype)


def rwkv_post(y, bonus, g, lnx_w, lnx_b, tm=256, tc=512):
    n = bonus.shape[0]
    one = pl.BlockSpec((tm, tc), lambda i, j: (i, j))
    ch = pl.BlockSpec((1, tc), lambda i, j: (0, j))
    return pl.pallas_call(
        _rwkv_post_kernel,
        grid=(n // tm, RW // tc),
        in_specs=[pl.BlockSpec((2, tm, tc), lambda i, j: (0, i, j)), one, one, ch, ch,
                  pl.BlockSpec((tc, tc), lambda i, j: (0, 0))],
        out_specs=one,
        out_shape=jax.ShapeDtypeStruct((n, RW), BF16),
        compiler_params=_cparams(("parallel", "parallel")),
        name="rwkv_post",
    )(y, bonus, g, lnx_w, lnx_b, _head_block_ones(tc))


def _ffn_act_kernel(xv, xv_p, xv_n, xg, xg_p, xg_n, cv, cg, bv, bg, o_ref, *, tm):
    row0 = pl.program_id(0) * tm

    def conv(x_ref, p_ref, n_ref, c_ref, b_ref):
        x = x_ref[...]
        xp, xn = _neighbours(x, p_ref[...], n_ref[...], row0)
        return xp * c_ref[0:1, :] + x * c_ref[1:2, :] + xn * c_ref[2:3, :] + b_ref[...]

    val = conv(xv, xv_p, xv_n, cv, bv)
    gate = conv(xg, xg_p, xg_n, cg, bg)
    o_ref[...] = (gate * jax.nn.sigmoid(gate) * val).astype(o_ref.dtype)


def ffn_act(u, conv_w, conv_b, tm=256, tc=512):
    n = u.shape[0]
    nrb = n // 8
    tb = tm // 8
    goff = D_FF_PAD // tc

    def main(o):
        return pl.BlockSpec((tm, tc), lambda i, j: (i, o + j))

    def prev(o):
        return pl.BlockSpec((8, tc), lambda i, j: (jnp.maximum(i * tb - 1, 0), o + j))

    def nxt(o):
        return pl.BlockSpec((8, tc), lambda i, j: (jnp.minimum((i + 1) * tb, nrb - 1), o + j))

    def chan(rows, o):
        return pl.BlockSpec((rows, tc), lambda i, j: (0, o + j))

    return pl.pallas_call(
        functools.partial(_ffn_act_kernel, tm=tm),
        grid=(n // tm, D_FF_PAD // tc),
        in_specs=[main(0), prev(0), nxt(0), main(goff), prev(goff), nxt(goff),
                  chan(3, 0), chan(3, goff), chan(1, 0), chan(1, goff)],
        out_specs=pl.BlockSpec((tm, tc), lambda i, j: (i, j)),
        out_shape=jax.ShapeDtypeStruct((n, D_FF_PAD), BF16),
        compiler_params=_cparams(("parallel", "parallel")),
        name="ffn_act",
    )(u, u, u, u, u, u, conv_w, conv_w, conv_b, conv_b)


def _pad_cols(w, at, width):
    return jnp.concatenate([w[..., :at], jnp.zeros(w.shape[:-1] + (width,), w.dtype), w[..., at:]], axis=-1)


def _layer_params(l, norm_mix, w_in, q_gain, k_gain, rwkv_conv, decay_w0, decay_w2, iclr_a0, iclr_a2,
                  gate_g2, k_k, k_a, r_k, lnx_w, lnx_b, w_up_attn, w_up_rwkv, w_o, norm_ffn,
                  w_ffn_up, ffn_conv, ffn_conv_b, w_ffn_down):
    pad_ff = D_FF_PAD - D_FF
    gl_end = 10208
    up = _pad_cols(_pad_cols(w_ffn_up[l], 2 * D_FF, pad_ff), D_FF, pad_ff)
    conv = _pad_cols(_pad_cols(ffn_conv[l], 2 * D_FF, pad_ff), D_FF, pad_ff)
    conv_b = _pad_cols(_pad_cols(ffn_conv_b[l][None, :], 2 * D_FF, pad_ff), D_FF, pad_ff)
    gains = jnp.stack([q_gain[l]] * 4 + [k_gain[l]] * 2)[:, None, :]
    return dict(
        norm_mix=norm_mix[l],
        w_in=_pad_cols(w_in[l], gl_end, GATE_LORA_PAD - GATE_LORA).astype(BF16),
        gains=gains,
        rwkv_conv=rwkv_conv[l],
        decay_w0=decay_w0[l], decay_w2=decay_w2[l].astype(BF16),
        iclr_a0=iclr_a0[l], iclr_a2=iclr_a2[l].astype(BF16),
        gate_g2=jnp.pad(gate_g2[l], ((0, GATE_LORA_PAD - GATE_LORA), (0, 0))).astype(BF16),
        k_k=k_k[l][None, :], k_a=k_a[l][None, :], r_k=r_k[l].reshape(1, RW),
        lnx_w=lnx_w[l][None, :], lnx_b=lnx_b[l][None, :],
        w_up_attn=w_up_attn[l].astype(BF16), w_up_rwkv=w_up_rwkv[l].astype(BF16),
        w_o=w_o[l].astype(BF16), norm_ffn=norm_ffn[l],
        w_ffn_up=up.astype(BF16), ffn_conv=conv, ffn_conv_b=conv_b,
        w_ffn_down=jnp.pad(w_ffn_down[l], ((0, pad_ff), (0, 0))).astype(BF16),
    )


def _layer(x, p, tables):
    h = rmsnorm_cast(x, p["norm_mix"])
    proj = matmul(h, p["w_in"], name="in_proj")
    qkv = qkv_prep(proj, p["gains"], tables)
    att = jnp.concatenate([attention(qkv, 0, P_BATCH, P_SEQ),
                           attention(qkv, N_PROMPT, S_BATCH, S_SEQ)], axis=0)
    r, v, kk, bonus, g, lw, kd, b = rwkv_prep(proj, p)
    y = rwkv_scan(r, v, kk, lw, kd, b)
    rw = rwkv_post(y, bonus, g, p["lnx_w"], p["lnx_b"])
    mixed = merge_branches(att, rw, p["w_up_attn"], p["w_up_rwkv"], proj)
    x = matmul(mixed, p["w_o"], res=x, name="out_proj")
    h = rmsnorm_cast(x, p["norm_ffn"])
    u = matmul(h, p["w_ffn_up"], name="ffn_up")
    act = ffn_act(u, p["ffn_conv"], p["ffn_conv_b"])
    return matmul_ktiled_res(act, p["w_ffn_down"], x, name="ffn_down")


def kernel(x_prompt, x_sample, norm_mix, w_in, q_gain, k_gain, rwkv_conv, decay_w0, decay_w2, iclr_a0, iclr_a2, gate_g2, k_k, k_a, r_k, lnx_w, lnx_b, w_up_attn, w_up_rwkv, w_o, norm_ffn, w_ffn_up, ffn_conv, ffn_conv_b, w_ffn_down):
    weights = (norm_mix, w_in, q_gain, k_gain, rwkv_conv, decay_w0, decay_w2, iclr_a0, iclr_a2,
               gate_g2, k_k, k_a, r_k, lnx_w, lnx_b, w_up_attn, w_up_rwkv, w_o, norm_ffn,
               w_ffn_up, ffn_conv, ffn_conv_b, w_ffn_down)
    x = jnp.concatenate([x_prompt.reshape(N_PROMPT, D_MODEL), x_sample.reshape(-1, D_MODEL)], axis=0)
    tables = _rope_tables()
    for l in range(norm_mix.shape[0]):
        x = _layer(x, _layer_params(l, *weights), tables)
    return (x[:N_PROMPT].reshape(P_BATCH, P_SEQ, D_MODEL), x[N_PROMPT:].reshape(S_BATCH, S_SEQ, D_MODEL))
```

```python
import functools

import jax
import jax.numpy as jnp
import numpy as np
from jax import lax
from jax.experimental import pallas as pl
from jax.experimental.pallas import tpu as pltpu

F32 = jnp.float32
BF16 = jnp.bfloat16
HI = lax.Precision.HIGHEST

D_MODEL = 4096
P_BATCH, P_SEQ = 4, 2048
S_BATCH, S_SEQ = 2, 4096
N_PROMPT = P_BATCH * P_SEQ
N_TOK = N_PROMPT + S_BATCH * S_SEQ
GRID_W = 64
HEAD = 128
ATT_W = 2048
KV_W = 512
KV_HEADS = 4
GROUPS = 4
ROPE_THETA = 10000.0
RK = 64
RW = 2048
LORA = 128
GATE_LORA = 480
GATE_LORA_PAD = 512
D_FF = 11008
D_FF_PAD = 11264
NORM_EPS = 1e-6
LNX_EPS = RK * 1e-5

C_Q, C_K, C_V = 0, 2048, 2560
C_R, C_RK, C_RV = 3072, 5120, 7168
C_WL, C_AL, C_GL = 9216, 9472, 9728
C_GA, C_GR = 10240, 14336
IN_PAD = 18432

CHUNK = 64
VMEM_LIMIT = 56 * 1024 * 1024


def _cparams(sem):
    return pltpu.CompilerParams(dimension_semantics=sem, vmem_limit_bytes=VMEM_LIMIT)


def _seq_edges(rows):
    pos = rows & jnp.where(rows < N_PROMPT, P_SEQ - 1, S_SEQ - 1)
    last = jnp.where(rows < N_PROMPT, P_SEQ - 1, S_SEQ - 1)
    return pos == 0, pos == last


def _neighbours(x, prev_row, next_row, row0):
    tm = x.shape[0]
    li = lax.broadcasted_iota(jnp.int32, x.shape, 0)
    start, end = _seq_edges(row0 + li)
    xp = jnp.where(li == 0, prev_row, pltpu.roll(x, 1, axis=0))
    xn = jnp.where(li == tm - 1, next_row, pltpu.roll(x, tm - 1, axis=0))
    return jnp.where(start, 0.0, xp), jnp.where(end, 0.0, xn)


def _rmsnorm_kernel(x_ref, g_ref, o_ref):
    x = x_ref[...]
    ms = jnp.mean(x * x, axis=-1, keepdims=True)
    o_ref[...] = (x * lax.rsqrt(ms + NORM_EPS) * g_ref[...]).astype(o_ref.dtype)


def rmsnorm_cast(x, g, tm=256):
    n, d = x.shape
    return pl.pallas_call(
        _rmsnorm_kernel,
        grid=(n // tm,),
        in_specs=[pl.BlockSpec((tm, d), lambda i: (i, 0)), pl.BlockSpec((1, d), lambda i: (0, 0))],
        out_specs=pl.BlockSpec((tm, d), lambda i: (i, 0)),
        out_shape=jax.ShapeDtypeStruct((n, d), BF16),
        compiler_params=_cparams(("parallel",)),
        name="rmsnorm_cast",
    )(x, g.reshape(1, d))


def _mm_kernel(a_ref, b_ref, o_ref):
    o_ref[...] = jnp.dot(a_ref[...], b_ref[...], preferred_element_type=F32).astype(o_ref.dtype)


def _mm_res_kernel(a_ref, b_ref, r_ref, o_ref):
    o_ref[...] = r_ref[...] + jnp.dot(a_ref[...], b_ref[...], preferred_element_type=F32)


def matmul(a, b, res=None, *, tm=512, tn=512, out_dtype=F32, name="matmul"):
    m, k = a.shape
    n = b.shape[1]
    in_specs = [pl.BlockSpec((tm, k), lambda j, i: (i, 0)), pl.BlockSpec((k, tn), lambda j, i: (0, j))]
    args = [a, b]
    kern = _mm_kernel
    if res is not None:
        in_specs.append(pl.BlockSpec((tm, tn), lambda j, i: (i, j)))
        args.append(res)
        kern = _mm_res_kernel
    return pl.pallas_call(
        kern,
        grid=(n // tn, m // tm),
        in_specs=in_specs,
        out_specs=pl.BlockSpec((tm, tn), lambda j, i: (i, j)),
        out_shape=jax.ShapeDtypeStruct((m, n), out_dtype),
        compiler_params=_cparams(("parallel", "parallel")),
        name=name,
    )(*args)


def _mmk_res_kernel(a_ref, b_ref, r_ref, o_ref, acc_ref):
    kk = pl.program_id(2)

    @pl.when(kk == 0)
    def _():
        acc_ref[...] = r_ref[...]

    acc_ref[...] += jnp.dot(a_ref[...], b_ref[...], preferred_element_type=F32)

    @pl.when(kk == pl.num_programs(2) - 1)
    def _():
        o_ref[...] = acc_ref[...]


def matmul_ktiled_res(a, b, res, *, tm=1024, tn=1024, tk=1408, name="matmul_k"):
    m, k = a.shape
    n = b.shape[1]
    return pl.pallas_call(
        _mmk_res_kernel,
        grid=(m // tm, n // tn, k // tk),
        in_specs=[
            pl.BlockSpec((tm, tk), lambda i, j, kk: (i, kk)),
            pl.BlockSpec((tk, tn), lambda i, j, kk: (kk, j)),
            pl.BlockSpec((tm, tn), lambda i, j, kk: (i, j)),
        ],
        out_specs=pl.BlockSpec((tm, tn), lambda i, j, kk: (i, j)),
        out_shape=jax.ShapeDtypeStruct((m, n), F32),
        scratch_shapes=[pltpu.VMEM((tm, tn), F32)],
        compiler_params=_cparams(("parallel", "parallel", "arbitrary")),
        name=name,
    )(a, b, res)


def _merge_kernel(att_ref, rw_ref, wa_ref, wr_ref, ga_ref, gr_ref, o_ref):
    ya = jnp.dot(att_ref[...], wa_ref[...], preferred_element_type=F32)
    yr = jnp.dot(rw_ref[...], wr_ref[...], preferred_element_type=F32)
    o_ref[...] = (jax.nn.sigmoid(ga_ref[...]) * ya + jax.nn.sigmoid(gr_ref[...]) * yr).astype(o_ref.dtype)


def merge_branches(att, rw, wa, wr, proj, tm=512, tn=512):
    n = att.shape[0]
    ga0, gr0 = C_GA // tn, C_GR // tn
    return pl.pallas_call(
        _merge_kernel,
        grid=(D_MODEL // tn, n // tm),
        in_specs=[
            pl.BlockSpec((tm, ATT_W), lambda j, i: (i, 0)),
            pl.BlockSpec((tm, RW), lambda j, i: (i, 0)),
            pl.BlockSpec((ATT_W, tn), lambda j, i: (0, j)),
            pl.BlockSpec((RW, tn), lambda j, i: (0, j)),
            pl.BlockSpec((tm, tn), lambda j, i: (i, ga0 + j)),
            pl.BlockSpec((tm, tn), lambda j, i: (i, gr0 + j)),
        ],
        out_specs=pl.BlockSpec((tm, tn), lambda j, i: (i, j)),
        out_shape=jax.ShapeDtypeStruct((n, D_MODEL), BF16),
        compiler_params=_cparams(("parallel", "parallel")),
        name="merge_branches",
    )(att, rw, wa, wr, proj, proj)


def _qkv_prep_kernel(x_ref, gain_ref, cos_ref, sa_ref, sb_ref, o_ref):
    j = pl.program_id(1)

    @pl.when(j < 5)
    def _():
        cos, sa, sb = cos_ref[...], sa_ref[...], sb_ref[...]
        gain = gain_ref[0]
        for h in range(4):
            x = x_ref[:, h * HEAD:(h + 1) * HEAD]
            ms = jnp.mean(x * x, axis=-1, keepdims=True)
            y = x * lax.rsqrt(ms + NORM_EPS) * gain
            y = y * cos + pltpu.roll(y, 96, axis=1) * sa + pltpu.roll(y, 32, axis=1) * sb
            o_ref[:, h * HEAD:(h + 1) * HEAD] = y.astype(o_ref.dtype)

    @pl.when(j == 5)
    def _():
        o_ref[...] = x_ref[...].astype(o_ref.dtype)


def _rope_tables():
    t = np.arange(S_SEQ)
    row = (t // GRID_W).astype(np.float32)
    col = (t % GRID_W).astype(np.float32)
    axis_dim = HEAD // 2
    inv = jnp.asarray(ROPE_THETA, F32) ** (-jnp.arange(0, axis_dim, 2, dtype=F32) / axis_dim)
    ang_r = jnp.asarray(row)[:, None] * inv
    ang_c = jnp.asarray(col)[:, None] * inv
    cr, sr, cc, sc = jnp.cos(ang_r), jnp.sin(ang_r), jnp.cos(ang_c), jnp.sin(ang_c)
    z = jnp.zeros_like(sr)
    cos = jnp.concatenate([cr, cr, cc, cc], axis=1)
    sin_a = jnp.concatenate([-sr, z, -sc, z], axis=1)
    sin_b = jnp.concatenate([z, sr, z, sc], axis=1)
    return cos, sin_a, sin_b


def qkv_prep(proj, gains, tables, tm=256):
    n = proj.shape[0]
    npb = N_PROMPT // tm

    def tab_map(i, j):
        return (jnp.where(i < npb, i % (P_SEQ // tm), (i - npb) % (S_SEQ // tm)), 0)

    tab_spec = pl.BlockSpec((tm, HEAD), tab_map)
    return pl.pallas_call(
        _qkv_prep_kernel,
        grid=(n // tm, 6),
        in_specs=[
            pl.BlockSpec((tm, 512), lambda i, j: (i, j)),
            pl.BlockSpec((1, 1, HEAD), lambda i, j: (j, 0, 0)),
            tab_spec, tab_spec, tab_spec,
        ],
        out_specs=pl.BlockSpec((tm, 512), lambda i, j: (i, j)),
        out_shape=jax.ShapeDtypeStruct((n, 3072), BF16),
        compiler_params=_cparams(("parallel", "parallel")),
        name="qkv_prep",
    )(proj, gains, *tables)


def _attn_kernel(q_ref, k_ref, v_ref, o_ref):
    k = k_ref[...]
    v = v_ref[...]
    scale = HEAD ** -0.5
    for g in range(GROUPS):
        q = q_ref[:, g * HEAD:(g + 1) * HEAD]
        s = lax.dot_general(q, k, (((1,), (1,)), ((), ())), preferred_element_type=F32) * scale
        m = jnp.max(s, axis=-1, keepdims=True)
        p = jnp.exp(s - m)
        l = jnp.sum(p, axis=-1, keepdims=True)
        o = jnp.dot(p.astype(BF16), v, preferred_element_type=F32)
        o_ref[:, g * HEAD:(g + 1) * HEAD] = (o / l).astype(o_ref.dtype)


def attention(qkv, row0, batch, seq, tq=256):
    qb0 = row0 // tq
    sb0 = row0 // seq
    nq = seq // tq
    return pl.pallas_call(
        _attn_kernel,
        grid=(batch, KV_HEADS, nq),
        in_specs=[
            pl.BlockSpec((tq, GROUPS * HEAD), lambda b, h, i: (qb0 + b * nq + i, h)),
            pl.BlockSpec((seq, HEAD), lambda b, h, i: (sb0 + b, C_K // HEAD + h)),
            pl.BlockSpec((seq, HEAD), lambda b, h, i: (sb0 + b, C_V // HEAD + h)),
        ],
        out_specs=pl.BlockSpec((tq, GROUPS * HEAD), lambda b, h, i: (b * nq + i, h)),
        out_shape=jax.ShapeDtypeStruct((batch * seq, ATT_W), BF16),
        compiler_params=_cparams(("parallel", "parallel", "arbitrary")),
        name="attention",
    )(qkv, qkv, qkv)


def _softplus(z):
    return jnp.maximum(z, 0.0) + jnp.log1p(jnp.exp(-jnp.abs(z)))


def _rwkv_prep_kernel(xr, xr_p, xr_n, xk, xk_p, xk_n, xv, xv_p, xv_n, wl_ref, al_ref, gl_ref,
                      cr, ck, cv, w0, w2, a0, a2, g2, kkw, kaw, rkw, bd_ref,
                      r_o, v_o, kk_o, bonus_o, g_o, lw_o, kd_o, b_o, *, tm):
    row0 = pl.program_id(0) * tm

    def conv(x_ref, p_ref, n_ref, c_ref):
        x = x_ref[...]
        xp, xn = _neighbours(x, p_ref[7:8, :], n_ref[0:1, :], row0)
        return xp * c_ref[0:1, :] + x * c_ref[1:2, :] + xn * c_ref[2:3, :]

    r = conv(xr, xr_p, xr_n, cr)
    k = conv(xk, xk_p, xk_n, ck)
    v = conv(xv, xv_p, xv_n, cv)
    bd = bd_ref[...]

    def segsum(z):
        return jnp.dot(z, bd, precision=HI, preferred_element_type=F32)

    kkr = k * kkw[...]
    kk = kkr / jnp.maximum(jnp.sqrt(segsum(kkr * kkr)), 1e-12)
    g = jnp.dot(jax.nn.sigmoid(gl_ref[...]).astype(BF16), g2[...], preferred_element_type=F32)
    r_o[...] = r
    v_o[...] = v
    kk_o[...] = kk
    g_o[...] = g
    bc = jnp.zeros_like(r)
    for d in range(2):
        wl = jnp.tanh(wl_ref[:, d * LORA:(d + 1) * LORA]).astype(BF16)
        w_raw = w0[d:d + 1, :] + jnp.dot(wl, w2[d], preferred_element_type=F32)
        lw_o[d] = -jnp.exp(-_softplus(-w_raw) - 0.5)
        al = al_ref[:, d * LORA:(d + 1) * LORA].astype(BF16)
        a = jax.nn.sigmoid(a0[d:d + 1, :] + jnp.dot(al, a2[d], preferred_element_type=F32))
        kd = k * (1.0 + (a - 1.0) * kaw[...])
        kd_o[d] = kd
        b_o[d] = a * kk
        bc = bc + r * kd * rkw[...]
    bonus_o[...] = segsum(bc) * v


def _head_block_ones(width):
    i = np.arange(width) // RK
    return jnp.asarray((i[:, None] == i[None, :]).astype(np.float32))


def rwkv_prep(proj, p, tm=256, tc=512):
    n = proj.shape[0]
    nrb = n // 8
    tb = tm // 8

    def main(c0):
        return pl.BlockSpec((tm, tc), lambda i, j: (i, c0 // tc + j))

    def prev(c0):
        return pl.BlockSpec((8, tc), lambda i, j: (jnp.maximum(i * tb - 1, 0), c0 // tc + j))

    def nxt(c0):
        return pl.BlockSpec((8, tc), lambda i, j: (jnp.minimum((i + 1) * tb, nrb - 1), c0 // tc + j))

    def chan(rows, c0=0):
        return pl.BlockSpec((rows, tc), lambda i, j: (0, c0 // tc + j))

    in_specs = []
    for c0 in (C_R, C_RK, C_RV):
        in_specs += [main(c0), prev(c0), nxt(c0)]
    in_specs += [
        pl.BlockSpec((tm, 2 * LORA), lambda i, j: (i, C_WL // (2 * LORA))),
        pl.BlockSpec((tm, 2 * LORA), lambda i, j: (i, C_AL // (2 * LORA))),
        pl.BlockSpec((tm, GATE_LORA_PAD), lambda i, j: (i, C_GL // GATE_LORA_PAD)),
        chan(3, 0), chan(3, RW), chan(3, 2 * RW),
        chan(2),
        pl.BlockSpec((2, LORA, tc), lambda i, j: (0, 0, j)),
        chan(2),
        pl.BlockSpec((2, LORA, tc), lambda i, j: (0, 0, j)),
        pl.BlockSpec((GATE_LORA_PAD, tc), lambda i, j: (0, j)),
        chan(1), chan(1), chan(1),
        pl.BlockSpec((tc, tc), lambda i, j: (0, 0)),
    ]
    one = pl.BlockSpec((tm, tc), lambda i, j: (i, j))
    two = pl.BlockSpec((2, tm, tc), lambda i, j: (0, i, j))
    s1 = jax.ShapeDtypeStruct((n, RW), F32)
    s2 = jax.ShapeDtypeStruct((2, n, RW), F32)
    return pl.pallas_call(
        functools.partial(_rwkv_prep_kernel, tm=tm),
        grid=(n // tm, RW // tc),
        in_specs=in_specs,
        out_specs=[one, one, one, one, one, two, two, two],
        out_shape=[s1, s1, s1, s1, s1, s2, s2, s2],
        compiler_params=_cparams(("parallel", "parallel")),
        name="rwkv_prep",
    )(proj, proj, proj, proj, proj, proj, proj, proj, proj, proj, proj, proj,
      p["rwkv_conv"], p["rwkv_conv"], p["rwkv_conv"], p["decay_w0"], p["decay_w2"], p["iclr_a0"],
      p["iclr_a2"], p["gate_g2"], p["k_k"], p["k_a"], p["r_k"], _head_block_ones(tc))


def _scan_kernel(r_ref, v_ref, kk_ref, lw_ref, kd_ref, b_ref, y_ref, s_ref, *, pg, nc):
    d = pl.program_id(0)
    c = pl.program_id(2)
    chunk = c + d * (nc - 1 - 2 * c)
    npc = N_PROMPT // CHUNK
    in_p = chunk < npc
    per = jnp.where(in_p, P_SEQ // CHUNK, S_SEQ // CHUNK)
    rel = jnp.where(in_p, chunk, chunk - npc)
    @pl.when((rel + d) % per == 0)
    def _():
        s_ref[...] = jnp.zeros_like(s_ref)

    sgn = 1 - 2 * d
    C = CHUNK
    t_i = lax.broadcasted_iota(jnp.int32, (C, 2 * C), 0)
    s_i = lax.broadcasted_iota(jnp.int32, (C, 2 * C), 1) & (C - 1)
    rel_ts = (t_i - s_i) * sgn
    strict = rel_ts > 0
    incl = rel_ts >= 0
    lane = lax.broadcasted_iota(jnp.int32, (C, 2 * C), 1)
    m_a = lane < C
    tt = lax.broadcasted_iota(jnp.int32, (C, C), 0)
    ss = lax.broadcasted_iota(jnp.int32, (C, C), 1)
    tri = jnp.where((tt - ss) * sgn >= 0, 1.0, 0.0).astype(BF16)
    ri = lax.broadcasted_iota(jnp.int32, (2 * C, 2 * C), 0)
    ci = lax.broadcasted_iota(jnp.int32, (2 * C, 2 * C), 1)
    same_head = (ri >> 6) == (ci >> 6)
    eye = jnp.where(ri == ci, 1.0, 0.0).astype(F32)
    same8 = (ri >> 3) == (ci >> 3)
    lvl = [((ri >> (s + 1)) == (ci >> (s + 1))) & ((ri >> s) != (ci >> s)) for s in (3, 4, 5)]

    def split(x):
        return jnp.concatenate([jnp.where(m_a, x, 0.0), jnp.where(m_a, 0.0, x)], axis=0)

    def bf(x):
        return x.astype(BF16)

    def mm(a, b):
        return jnp.dot(a, b, preferred_element_type=F32)

    def mm_nt(a, b):
        return lax.dot_general(a, b, (((1,), (1,)), ((), ())), preferred_element_type=F32)

    def mm_tn(a, b):
        return lax.dot_general(a, b, (((0,), (0,)), ((), ())), preferred_element_type=F32)

    def cumsum(lw):
        hi = bf(lw)
        r1 = lw - hi.astype(F32)
        mid = bf(r1)
        lo = bf(r1 - mid.astype(F32))
        return mm(tri, hi) + (mm(tri, mid) + mm(tri, lo))

    P = range(pg)
    sls = [slice(p * 2 * C, (p + 1) * 2 * C) for p in P]
    r = [r_ref[:, sl] for sl in sls]
    v = [v_ref[:, sl] for sl in sls]
    kk = [kk_ref[:, sl] for sl in sls]
    lw = [lw_ref[0, :, sl] for sl in sls]
    kd = [kd_ref[0, :, sl] for sl in sls]
    b = [b_ref[0, :, sl] for sl in sls]
    cl = [cumsum(lw[p]) for p in P]
    tot = [jnp.sum(lw[p], axis=0, keepdims=True) for p in P]
    e_neg = [jnp.exp(-cl[p]) for p in P]
    e_end = [jnp.exp(tot[p] - cl[p]) for p in P]
    x = [bf(jnp.concatenate([r[p] * jnp.exp(cl[p]), kk[p] * jnp.exp(cl[p] - lw[p])], axis=0)) for p in P]
    z = [bf(jnp.concatenate([split(kd[p] * e_neg[p]), split(b[p] * e_neg[p])], axis=0)) for p in P]
    pm = [mm_nt(x[p], z[p]) for p in P]
    ld = [split(jnp.where(strict, pm[p][C:2 * C, 2 * C:4 * C], 0.0)) for p in P]
    l8 = [jnp.where(same8, ld[p], 0.0) for p in P]
    l8b = [bf(l8[p]) for p in P]
    l8_2 = [mm(l8b[p], l8b[p]) for p in P]
    l8_2b = [bf(l8_2[p]) for p in P]
    l8_4 = [mm(l8_2b[p], l8_2b[p]) for p in P]
    t1 = [mm(bf(eye - l8[p]), bf(eye + l8_2[p])) for p in P]
    inv = [mm(bf(t1[p]), bf(eye + l8_4[p])) for p in P]
    for msk in lvl:
        invb = [bf(inv[p]) for p in P]
        t2 = [mm(invb[p], bf(jnp.where(msk, ld[p], 0.0))) for p in P]
        inv = [inv[p] - mm(bf(t2[p]), invb[p]) for p in P]
    invb = [bf(inv[p]) for p in P]
    st = [s_ref[p] for p in P]
    xs = [mm_nt(x[p], bf(st[p])) for p in P]
    vs = [bf(split(v[p])) for p in P]
    rhs = [xs[p][C:2 * C] + mm(bf(jnp.where(strict, pm[p][C:2 * C, 0:2 * C], 0.0)), vs[p]) for p in P]
    ud = [mm(invb[p], bf(split(rhs[p]))) for p in P]
    for p in P:
        coef = jnp.concatenate([jnp.where(incl, pm[p][0:C, 0:2 * C], 0.0),
                                jnp.where(incl, -pm[p][0:C, 2 * C:4 * C], 0.0)], axis=1)
        y_ref[0, :, sls[p]] = xs[p][0:C] + mm(bf(coef), jnp.concatenate([vs[p], bf(ud[p])], axis=0))
    for p in P:
        u = ud[p][0:C] + ud[p][C:2 * C]
        upd = mm_tn(bf(jnp.concatenate([v[p], u], axis=0)),
                    bf(jnp.concatenate([kd[p] * e_end[p], -(b[p] * e_end[p])], axis=0)))
        s_ref[p] = st[p] * jnp.exp(tot[p]) + jnp.where(same_head, upd, 0.0)


def rwkv_scan(r, v, kk, lw, kd, b, pg=8):
    n = r.shape[0]
    nc = n // CHUNK
    w = pg * 2 * CHUNK

    def cmap(d, g, c):
        return (c + d * (nc - 1 - 2 * c), g)

    def dmap(d, g, c):
        return (d, c + d * (nc - 1 - 2 * c), g)

    one = pl.BlockSpec((CHUNK, w), cmap)
    two = pl.BlockSpec((1, CHUNK, w), dmap)
    return pl.pallas_call(
        functools.partial(_scan_kernel, pg=pg, nc=nc),
        grid=(2, RW // w, nc),
        in_specs=[one, one, one, two, two, two],
        out_specs=two,
        out_shape=jax.ShapeDtypeStruct((2, n, RW), F32),
        scratch_shapes=[pltpu.VMEM((pg, 2 * CHUNK, 2 * CHUNK), F32)],
        compiler_params=_cparams(("parallel", "parallel", "arbitrary")),
        name="rwkv_scan",
    )(r, v, kk, lw, kd, b)


def _rwkv_post_kernel(y_ref, bonus_ref, g_ref, w_ref, b_ref, bd_ref, o_ref):
    bd = bd_ref[...]

    def segmean(z):
        return jnp.dot(z, bd, precision=HI, preferred_element_type=F32) * (1.0 / RK)

    y = y_ref[0] + y_ref[1]
    yc = y - segmean(y)
    var = segmean(yc * yc)
    out = yc * lax.rsqrt(var + LNX_EPS) * w_ref[...] + b_ref[...]
    o_ref[...] = ((out + bonus_ref[...]) * g_ref[...]).astype(o_ref.dtype)


def rwkv_post(y, bonus, g, lnx_w, lnx_b, tm=256, tc=512):
    n = bonus.shape[0]
    one = pl.BlockSpec((tm, tc), lambda i, j: (i, j))
    ch = pl.BlockSpec((1, tc), lambda i, j: (0, j))
    return pl.pallas_call(
        _rwkv_post_kernel,
        grid=(n // tm, RW // tc),
        in_specs=[pl.BlockSpec((2, tm, tc), lambda i, j: (0, i, j)), one, one, ch, ch,
                  pl.BlockSpec((tc, tc), lambda i, j: (0, 0))],
        out_specs=one,
        out_shape=jax.ShapeDtypeStruct((n, RW), BF16),
        compiler_params=_cparams(("parallel", "parallel")),
        name="rwkv_post",
    )(y, bonus, g, lnx_w, lnx_b, _head_block_ones(tc))


HALO = 16


def _ffn_up_kernel(h_ref, hp_ref, hn_ref, wv_ref, wg_ref, cv, cg, bv, bg, o_ref, *, tm):
    row0 = pl.program_id(1) * tm
    h = h_ref[...]
    edge = jnp.concatenate([hp_ref[...], hn_ref[...]], axis=0)

    def branch(w_ref, c_ref, b_ref):
        w = w_ref[...]
        u = jnp.dot(h, w, preferred_element_type=F32)
        ue = jnp.dot(edge, w, preferred_element_type=F32)
        up, un = _neighbours(u, ue[HALO - 1:HALO, :], ue[HALO:HALO + 1, :], row0)
        return up * c_ref[0:1, :] + u * c_ref[1:2, :] + un * c_ref[2:3, :] + b_ref[...]

    val = branch(wv_ref, cv, bv)
    gate = branch(wg_ref, cg, bg)
    o_ref[...] = (gate * jax.nn.sigmoid(gate) * val).astype(o_ref.dtype)


def ffn_up_act(h, w_up, conv_w, conv_b, tm=512, tn=512):
    n, k = h.shape
    nhb = n // HALO
    tb = tm // HALO
    goff = D_FF_PAD // tn

    def chan(rows, o):
        return pl.BlockSpec((rows, tn), lambda j, i: (0, o + j))

    return pl.pallas_call(
        functools.partial(_ffn_up_kernel, tm=tm),
        grid=(D_FF_PAD // tn, n // tm),
        in_specs=[
            pl.BlockSpec((tm, k), lambda j, i: (i, 0)),
            pl.BlockSpec((HALO, k), lambda j, i: (jnp.maximum(i * tb - 1, 0), 0)),
            pl.BlockSpec((HALO, k), lambda j, i: (jnp.minimum((i + 1) * tb, nhb - 1), 0)),
            pl.BlockSpec((k, tn), lambda j, i: (0, j)),
            pl.BlockSpec((k, tn), lambda j, i: (0, goff + j)),
            chan(3, 0), chan(3, goff), chan(1, 0), chan(1, goff),
        ],
        out_specs=pl.BlockSpec((tm, tn), lambda j, i: (i, j)),
        out_shape=jax.ShapeDtypeStruct((n, D_FF_PAD), BF16),
        compiler_params=_cparams(("parallel", "parallel")),
        name="ffn_up_act",
    )(h, h, h, w_up, w_up, conv_w, conv_w, conv_b, conv_b)


def _pad_cols(w, at, width):
    return jnp.concatenate([w[..., :at], jnp.zeros(w.shape[:-1] + (width,), w.dtype), w[..., at:]], axis=-1)


def _layer_params(l, norm_mix, w_in, q_gain, k_gain, rwkv_conv, decay_w0, decay_w2, iclr_a0, iclr_a2,
                  gate_g2, k_k, k_a, r_k, lnx_w, lnx_b, w_up_attn, w_up_rwkv, w_o, norm_ffn,
                  w_ffn_up, ffn_conv, ffn_conv_b, w_ffn_down):
    pad_ff = D_FF_PAD - D_FF
    gl_end = 10208
    up = _pad_cols(_pad_cols(w_ffn_up[l], 2 * D_FF, pad_ff), D_FF, pad_ff)
    conv = _pad_cols(_pad_cols(ffn_conv[l], 2 * D_FF, pad_ff), D_FF, pad_ff)
    conv_b = _pad_cols(_pad_cols(ffn_conv_b[l][None, :], 2 * D_FF, pad_ff), D_FF, pad_ff)
    gains = jnp.stack([q_gain[l]] * 4 + [k_gain[l]] * 2)[:, None, :]
    return dict(
        norm_mix=norm_mix[l],
        w_in=_pad_cols(w_in[l], gl_end, GATE_LORA_PAD - GATE_LORA).astype(BF16),
        gains=gains,
        rwkv_conv=rwkv_conv[l],
        decay_w0=decay_w0[l], decay_w2=decay_w2[l].astype(BF16),
        iclr_a0=iclr_a0[l], iclr_a2=iclr_a2[l].astype(BF16),
        gate_g2=jnp.pad(gate_g2[l], ((0, GATE_LORA_PAD - GATE_LORA), (0, 0))).astype(BF16),
        k_k=k_k[l][None, :], k_a=k_a[l][None, :], r_k=r_k[l].reshape(1, RW),
        lnx_w=lnx_w[l][None, :], lnx_b=lnx_b[l][None, :],
        w_up_attn=w_up_attn[l].astype(BF16), w_up_rwkv=w_up_rwkv[l].astype(BF16),
        w_o=w_o[l].astype(BF16), norm_ffn=norm_ffn[l],
        w_ffn_up=up.astype(BF16), ffn_conv=conv, ffn_conv_b=conv_b,
        w_ffn_down=jnp.pad(w_ffn_down[l], ((0, pad_ff), (0, 0))).astype(BF16),
    )


def _layer(x, p, tables):
    h = rmsnorm_cast(x, p["norm_mix"])
    proj = matmul(h, p["w_in"], name="in_proj")
    qkv = qkv_prep(proj, p["gains"], tables)
    att = jnp.concatenate([attention(qkv, 0, P_BATCH, P_SEQ),
                           attention(qkv, N_PROMPT, S_BATCH, S_SEQ)], axis=0)
    r, v, kk, bonus, g, lw, kd, b = rwkv_prep(proj, p)
    y = rwkv_scan(r, v, kk, lw, kd, b)
    rw = rwkv_post(y, bonus, g, p["lnx_w"], p["lnx_b"])
    mixed = merge_branches(att, rw, p["w_up_attn"], p["w_up_rwkv"], proj)
    x = matmul(mixed, p["w_o"], res=x, name="out_proj")
    h = rmsnorm_cast(x, p["norm_ffn"])
    act = ffn_up_act(h, p["w_ffn_up"], p["ffn_conv"], p["ffn_conv_b"])
    return matmul_ktiled_res(act, p["w_ffn_down"], x, name="ffn_down")


def kernel(x_prompt, x_sample, norm_mix, w_in, q_gain, k_gain, rwkv_conv, decay_w0, decay_w2, iclr_a0, iclr_a2, gate_g2, k_k, k_a, r_k, lnx_w, lnx_b, w_up_attn, w_up_rwkv, w_o, norm_ffn, w_ffn_up, ffn_conv, ffn_conv_b, w_ffn_down):
    weights = (norm_mix, w_in, q_gain, k_gain, rwkv_conv, decay_w0, decay_w2, iclr_a0, iclr_a2,
               gate_g2, k_k, k_a, r_k, lnx_w, lnx_b, w_up_attn, w_up_rwkv, w_o, norm_ffn,
               w_ffn_up, ffn_conv, ffn_conv_b, w_ffn_down)
    x = jnp.concatenate([x_prompt.reshape(N_PROMPT, D_MODEL), x_sample.reshape(-1, D_MODEL)], axis=0)
    tables = _rope_tables()
    for l in range(norm_mix.shape[0]):
        x = _layer(x, _layer_params(l, *weights), tables)
    return (x[:N_PROMPT].reshape(P_BATCH, P_SEQ, D_MODEL), x[N_PROMPT:].reshape(S_BATCH, S_SEQ, D_MODEL))
```

```python
import functools

import jax
import jax.numpy as jnp
import numpy as np
from jax import lax
from jax.experimental import pallas as pl
from jax.experimental.pallas import tpu as pltpu

F32 = jnp.float32
BF16 = jnp.bfloat16

D_MODEL = 4096
P_BATCH, P_SEQ = 4, 2048
S_BATCH, S_SEQ = 2, 4096
N_PROMPT = P_BATCH * P_SEQ
N_TOK = N_PROMPT + S_BATCH * S_SEQ
GRID_W = 64
HEAD = 128
ATT_W = 2048
KV_W = 512
KV_HEADS = 4
GROUPS = 4
ROPE_THETA = 10000.0
RK = 64
RW = 2048
LORA = 128
GATE_LORA = 480
GATE_LORA_PAD = 512
D_FF = 11008
D_FF_PAD = 11264
NORM_EPS = 1e-6
LNX_EPS = RK * 1e-5

C_Q, C_K, C_V = 0, 2048, 2560
C_R, C_RK, C_RV = 3072, 5120, 7168
C_WL, C_AL, C_GL = 9216, 9472, 9728
C_GA = C_GL + GATE_LORA_PAD
IN_GATES = C_GL + GATE_LORA

CHUNK = 64
VMEM_LIMIT = 56 * 1024 * 1024


def _cparams(sem):
    return pltpu.CompilerParams(dimension_semantics=sem, vmem_limit_bytes=VMEM_LIMIT)


def _seq_edges(rows):
    pos = rows & jnp.where(rows < N_PROMPT, P_SEQ - 1, S_SEQ - 1)
    last = jnp.where(rows < N_PROMPT, P_SEQ - 1, S_SEQ - 1)
    return pos == 0, pos == last


def _neighbours(x, prev_row, next_row, row0):
    tm = x.shape[0]
    li = lax.broadcasted_iota(jnp.int32, x.shape, 0)
    start, end = _seq_edges(row0 + li)
    xp = jnp.where(li == 0, prev_row, pltpu.roll(x, 1, axis=0))
    xn = jnp.where(li == tm - 1, next_row, pltpu.roll(x, tm - 1, axis=0))
    return jnp.where(start, 0.0, xp), jnp.where(end, 0.0, xn)


def _rmsnorm_kernel(x_ref, g_ref, o_ref):
    x = x_ref[...]
    ms = jnp.mean(x * x, axis=-1, keepdims=True)
    o_ref[...] = (x * lax.rsqrt(ms + NORM_EPS) * g_ref[...]).astype(o_ref.dtype)


def rmsnorm_cast(x, g, tm=256):
    n, d = x.shape
    return pl.pallas_call(
        _rmsnorm_kernel,
        grid=(n // tm,),
        in_specs=[pl.BlockSpec((tm, d), lambda i: (i, 0)), pl.BlockSpec((1, d), lambda i: (0, 0))],
        out_specs=pl.BlockSpec((tm, d), lambda i: (i, 0)),
        out_shape=jax.ShapeDtypeStruct((n, d), BF16),
        compiler_params=_cparams(("parallel",)),
        name="rmsnorm_cast",
    )(x, g.reshape(1, d))


def _cast_weight_once(w_ref, wb_ref):
    @pl.when(pl.program_id(1) == 0)
    def _():
        wb_ref[...] = w_ref[...].astype(BF16)


def _mm_kernel(a_ref, b_ref, o_ref, bb_ref):
    _cast_weight_once(b_ref, bb_ref)
    o_ref[...] = jnp.dot(a_ref[...], bb_ref[...], preferred_element_type=F32).astype(o_ref.dtype)


def _mm_res_kernel(a_ref, b_ref, r_ref, o_ref, bb_ref):
    _cast_weight_once(b_ref, bb_ref)
    o_ref[...] = r_ref[...] + jnp.dot(a_ref[...], bb_ref[...], preferred_element_type=F32)


def matmul(a, w, layer, n_cols, res=None, *, tm=512, tn=512, name="matmul"):
    m, k = a.shape
    in_specs = [pl.BlockSpec((tm, k), lambda j, i: (i, 0)),
                pl.BlockSpec((None, k, tn), lambda j, i: (layer, 0, j))]
    args = [a, w]
    kern = _mm_kernel
    if res is not None:
        in_specs.append(pl.BlockSpec((tm, tn), lambda j, i: (i, j)))
        args.append(res)
        kern = _mm_res_kernel
    return pl.pallas_call(
        kern,
        grid=(n_cols // tn, m // tm),
        in_specs=in_specs,
        out_specs=pl.BlockSpec((tm, tn), lambda j, i: (i, j)),
        out_shape=jax.ShapeDtypeStruct((m, n_cols), F32),
        scratch_shapes=[pltpu.VMEM((k, tn), BF16)],
        compiler_params=_cparams(("parallel", "arbitrary")),
        name=name,
    )(*args)


def _mmk_res_kernel(a_ref, b_ref, r_ref, o_ref, acc_ref):
    kk = pl.program_id(2)

    @pl.when(kk == 0)
    def _():
        acc_ref[...] = r_ref[...]

    acc_ref[...] += jnp.dot(a_ref[...], b_ref[...], preferred_element_type=F32)

    @pl.when(kk == pl.num_programs(2) - 1)
    def _():
        o_ref[...] = acc_ref[...]


def matmul_ktiled_res(a, b, res, *, tm=1024, tn=1024, tk=1408, name="matmul_k"):
    m, k = a.shape
    n = b.shape[1]
    return pl.pallas_call(
        _mmk_res_kernel,
        grid=(m // tm, n // tn, k // tk),
        in_specs=[
            pl.BlockSpec((tm, tk), lambda i, j, kk: (i, kk)),
            pl.BlockSpec((tk, tn), lambda i, j, kk: (kk, j)),
            pl.BlockSpec((tm, tn), lambda i, j, kk: (i, j)),
        ],
        out_specs=pl.BlockSpec((tm, tn), lambda i, j, kk: (i, j)),
        out_shape=jax.ShapeDtypeStruct((m, n), F32),
        scratch_shapes=[pltpu.VMEM((tm, tn), F32)],
        compiler_params=_cparams(("parallel", "parallel", "arbitrary")),
        name=name,
    )(a, b, res)


def _merge_kernel(att_ref, rw_ref, wa_ref, wr_ref, ga_ref, gr_ref, o_ref, wab_ref, wrb_ref):
    _cast_weight_once(wa_ref, wab_ref)
    _cast_weight_once(wr_ref, wrb_ref)
    ya = jnp.dot(att_ref[...], wab_ref[...], preferred_element_type=F32)
    yr = jnp.dot(rw_ref[...], wrb_ref[...], preferred_element_type=F32)
    o_ref[...] = (jax.nn.sigmoid(ga_ref[...]) * ya + jax.nn.sigmoid(gr_ref[...]) * yr).astype(o_ref.dtype)


def merge_branches(att, rw, wa, wr, layer, gates, tm=512, tn=512):
    n = att.shape[0]
    gr0 = D_MODEL // tn
    return pl.pallas_call(
        _merge_kernel,
        grid=(D_MODEL // tn, n // tm),
        in_specs=[
            pl.BlockSpec((tm, ATT_W), lambda j, i: (i, 0)),
            pl.BlockSpec((tm, RW), lambda j, i: (i, 0)),
            pl.BlockSpec((None, ATT_W, tn), lambda j, i: (layer, 0, j)),
            pl.BlockSpec((None, RW, tn), lambda j, i: (layer, 0, j)),
            pl.BlockSpec((tm, tn), lambda j, i: (i, j)),
            pl.BlockSpec((tm, tn), lambda j, i: (i, gr0 + j)),
        ],
        out_specs=pl.BlockSpec((tm, tn), lambda j, i: (i, j)),
        out_shape=jax.ShapeDtypeStruct((n, D_MODEL), BF16),
        scratch_shapes=[pltpu.VMEM((ATT_W, tn), BF16), pltpu.VMEM((RW, tn), BF16)],
        compiler_params=_cparams(("parallel", "arbitrary")),
        name="merge_branches",
    )(att, rw, wa, wr, gates, gates)


def _qkv_prep_kernel(x_ref, gain_ref, cos_ref, sa_ref, sb_ref, o_ref):
    j = pl.program_id(1)

    @pl.when(j < 5)
    def _():
        cos, sa, sb = cos_ref[...], sa_ref[...], sb_ref[...]
        gain = gain_ref[0]
        for h in range(4):
            x = x_ref[:, h * HEAD:(h + 1) * HEAD]
            ms = jnp.mean(x * x, axis=-1, keepdims=True)
            y = x * lax.rsqrt(ms + NORM_EPS) * gain
            y = y * cos + pltpu.roll(y, 96, axis=1) * sa + pltpu.roll(y, 32, axis=1) * sb
            o_ref[:, h * HEAD:(h + 1) * HEAD] = y.astype(o_ref.dtype)

    @pl.when(j == 5)
    def _():
        o_ref[...] = x_ref[...].astype(o_ref.dtype)


def _rope_tables():
    t = np.arange(S_SEQ)
    row = (t // GRID_W).astype(np.float32)
    col = (t % GRID_W).astype(np.float32)
    axis_dim = HEAD // 2
    inv = jnp.asarray(ROPE_THETA, F32) ** (-jnp.arange(0, axis_dim, 2, dtype=F32) / axis_dim)
    ang_r = jnp.asarray(row)[:, None] * inv
    ang_c = jnp.asarray(col)[:, None] * inv
    cr, sr, cc, sc = jnp.cos(ang_r), jnp.sin(ang_r), jnp.cos(ang_c), jnp.sin(ang_c)
    z = jnp.zeros_like(sr)
    cos = jnp.concatenate([cr, cr, cc, cc], axis=1)
    sin_a = jnp.concatenate([-sr, z, -sc, z], axis=1)
    sin_b = jnp.concatenate([z, sr, z, sc], axis=1)
    return cos, sin_a, sin_b


def qkv_prep(proj, gains, tables, tm=256):
    n = proj.shape[0]
    npb = N_PROMPT // tm

    def tab_map(i, j):
        return (jnp.where(i < npb, i % (P_SEQ // tm), (i - npb) % (S_SEQ // tm)), 0)

    tab_spec = pl.BlockSpec((tm, HEAD), tab_map)
    return pl.pallas_call(
        _qkv_prep_kernel,
        grid=(n // tm, 6),
        in_specs=[
            pl.BlockSpec((tm, 512), lambda i, j: (i, j)),
            pl.BlockSpec((1, 1, HEAD), lambda i, j: (j, 0, 0)),
            tab_spec, tab_spec, tab_spec,
        ],
        out_specs=pl.BlockSpec((tm, 512), lambda i, j: (i, j)),
        out_shape=jax.ShapeDtypeStruct((n, 3072), BF16),
        compiler_params=_cparams(("parallel", "parallel")),
        name="qkv_prep",
    )(proj, gains, *tables)


def _attn_kernel(q_ref, k_ref, v_ref, o_ref):
    k = k_ref[...]
    v = v_ref[...]
    scale = HEAD ** -0.5
    for g in range(GROUPS):
        q = q_ref[:, g * HEAD:(g + 1) * HEAD]
        s = lax.dot_general(q, k, (((1,), (1,)), ((), ())), preferred_element_type=F32) * scale
        m = jnp.max(s, axis=-1, keepdims=True)
        p = jnp.exp(s - m)
        l = jnp.sum(p, axis=-1, keepdims=True)
        o = jnp.dot(p.astype(BF16), v, preferred_element_type=F32)
        o_ref[:, g * HEAD:(g + 1) * HEAD] = (o / l).astype(o_ref.dtype)


def attention(qkv, row0, batch, seq, tq=256):
    qb0 = row0 // tq
    sb0 = row0 // seq
    nq = seq // tq
    return pl.pallas_call(
        _attn_kernel,
        grid=(batch, KV_HEADS, nq),
        in_specs=[
            pl.BlockSpec((tq, GROUPS * HEAD), lambda b, h, i: (qb0 + b * nq + i, h)),
            pl.BlockSpec((seq, HEAD), lambda b, h, i: (sb0 + b, C_K // HEAD + h)),
            pl.BlockSpec((seq, HEAD), lambda b, h, i: (sb0 + b, C_V // HEAD + h)),
        ],
        out_specs=pl.BlockSpec((tq, GROUPS * HEAD), lambda b, h, i: (b * nq + i, h)),
        out_shape=jax.ShapeDtypeStruct((batch * seq, ATT_W), BF16),
        compiler_params=_cparams(("parallel", "parallel", "arbitrary")),
        name="attention",
    )(qkv, qkv, qkv)


def _softplus(z):
    return jnp.maximum(z, 0.0) + jnp.log1p(jnp.exp(-jnp.abs(z)))


def _rwkv_prep_kernel(xr, xr_p, xr_n, xk, xk_p, xk_n, xv, xv_p, xv_n, wl_ref, al_ref, gl_ref,
                      cr, ck, cv, w0, w2, a0, a2, g2, kkw, kaw, rkw, bd_ref,
                      r_o, v_o, kk_o, bonus_o, g_o, lw_o, kd_o, b_o, *, tm):
    row0 = pl.program_id(0) * tm

    def conv(x_ref, p_ref, n_ref, c_ref):
        x = x_ref[...]
        xp, xn = _neighbours(x, p_ref[7:8, :], n_ref[0:1, :], row0)
        return xp * c_ref[0:1, :] + x * c_ref[1:2, :] + xn * c_ref[2:3, :]

    r = conv(xr, xr_p, xr_n, cr)
    k = conv(xk, xk_p, xk_n, ck)
    v = conv(xv, xv_p, xv_n, cv)
    bd = bd_ref[...]

    def segsum(z):
        return _head_sum(z, bd)

    kkr = k * kkw[...]
    kk = kkr / jnp.maximum(jnp.sqrt(segsum(kkr * kkr)), 1e-12)
    g = jnp.dot(jax.nn.sigmoid(gl_ref[...]).astype(BF16), g2[...], preferred_element_type=F32)
    r_o[...] = r
    v_o[...] = v
    kk_o[...] = kk
    g_o[...] = g
    bc = jnp.zeros_like(r)
    for d in range(2):
        wl = jnp.tanh(wl_ref[:, d * LORA:(d + 1) * LORA]).astype(BF16)
        w_raw = w0[d:d + 1, :] + jnp.dot(wl, w2[d], preferred_element_type=F32)
        lw_o[d] = -jnp.exp(-_softplus(-w_raw) - 0.5)
        al = al_ref[:, d * LORA:(d + 1) * LORA].astype(BF16)
        a = jax.nn.sigmoid(a0[d:d + 1, :] + jnp.dot(al, a2[d], preferred_element_type=F32))
        kd = k * (1.0 + (a - 1.0) * kaw[...])
        kd_o[d] = kd
        b_o[d] = a * kk
        bc = bc + r * kd * rkw[...]
    bonus_o[...] = segsum(bc) * v


def _head_block_ones(width):
    i = np.arange(width) // RK
    return jnp.asarray((i[:, None] == i[None, :]).astype(np.float32)).astype(BF16)


def _head_sum(z, ones_bd):
    hi = z.astype(BF16)
    lo = (z - hi.astype(F32)).astype(BF16)
    return (jnp.dot(hi, ones_bd, preferred_element_type=F32)
            + jnp.dot(lo, ones_bd, preferred_element_type=F32))


def rwkv_prep(proj, p, tm=256, tc=512):
    n = proj.shape[0]
    nrb = n // 8
    tb = tm // 8

    def main(c0):
        return pl.BlockSpec((tm, tc), lambda i, j: (i, c0 // tc + j))

    def prev(c0):
        return pl.BlockSpec((8, tc), lambda i, j: (jnp.maximum(i * tb - 1, 0), c0 // tc + j))

    def nxt(c0):
        return pl.BlockSpec((8, tc), lambda i, j: (jnp.minimum((i + 1) * tb, nrb - 1), c0 // tc + j))

    def chan(rows, c0=0):
        return pl.BlockSpec((rows, tc), lambda i, j: (0, c0 // tc + j))

    in_specs = []
    for c0 in (C_R, C_RK, C_RV):
        in_specs += [main(c0), prev(c0), nxt(c0)]
    in_specs += [
        pl.BlockSpec((tm, 2 * LORA), lambda i, j: (i, C_WL // (2 * LORA))),
        pl.BlockSpec((tm, 2 * LORA), lambda i, j: (i, C_AL // (2 * LORA))),
        pl.BlockSpec((tm, GATE_LORA_PAD), lambda i, j: (i, C_GL // GATE_LORA_PAD)),
        chan(3, 0), chan(3, RW), chan(3, 2 * RW),
        chan(2),
        pl.BlockSpec((2, LORA, tc), lambda i, j: (0, 0, j)),
        chan(2),
        pl.BlockSpec((2, LORA, tc), lambda i, j: (0, 0, j)),
        pl.BlockSpec((GATE_LORA_PAD, tc), lambda i, j: (0, j)),
        chan(1), chan(1), chan(1),
        pl.BlockSpec((tc, tc), lambda i, j: (0, 0)),
    ]
    one = pl.BlockSpec((tm, tc), lambda i, j: (i, j))
    two = pl.BlockSpec((2, tm, tc), lambda i, j: (0, i, j))
    s1 = jax.ShapeDtypeStruct((n, RW), F32)
    s2 = jax.ShapeDtypeStruct((2, n, RW), F32)
    return pl.pallas_call(
        functools.partial(_rwkv_prep_kernel, tm=tm),
        grid=(n // tm, RW // tc),
        in_specs=in_specs,
        out_specs=[one, one, one, one, one, two, two, two],
        out_shape=[s1, s1, s1, s1, s1, s2, s2, s2],
        compiler_params=_cparams(("parallel", "parallel")),
        name="rwkv_prep",
    )(proj, proj, proj, proj, proj, proj, proj, proj, proj, proj, proj, proj,
      p["rwkv_conv"], p["rwkv_conv"], p["rwkv_conv"], p["decay_w0"], p["decay_w2"], p["iclr_a0"],
      p["iclr_a2"], p["gate_g2"], p["k_k"], p["k_a"], p["r_k"], _head_block_ones(tc))


def _scan_kernel(r_ref, v_ref, kk_ref, lw_ref, kd_ref, b_ref, y_ref, s_ref, *, pg, nc):
    d = pl.program_id(0)
    c = pl.program_id(2)
    chunk = c + d * (nc - 1 - 2 * c)
    npc = N_PROMPT // CHUNK
    in_p = chunk < npc
    per = jnp.where(in_p, P_SEQ // CHUNK, S_SEQ // CHUNK)
    rel = jnp.where(in_p, chunk, chunk - npc)
    @pl.when((rel + d) % per == 0)
    def _():
        s_ref[...] = jnp.zeros_like(s_ref)

    sgn = 1 - 2 * d
    C = CHUNK
    t_i = lax.broadcasted_iota(jnp.int32, (C, 2 * C), 0)
    s_i = lax.broadcasted_iota(jnp.int32, (C, 2 * C), 1) & (C - 1)
    rel_ts = (t_i - s_i) * sgn
    strict = rel_ts > 0
    incl = rel_ts >= 0
    lane = lax.broadcasted_iota(jnp.int32, (C, 2 * C), 1)
    m_a = lane < C
    tt = lax.broadcasted_iota(jnp.int32, (C, C), 0)
    ss = lax.broadcasted_iota(jnp.int32, (C, C), 1)
    tri = jnp.where((tt - ss) * sgn >= 0, 1.0, 0.0).astype(BF16)
    ri = lax.broadcasted_iota(jnp.int32, (2 * C, 2 * C), 0)
    ci = lax.broadcasted_iota(jnp.int32, (2 * C, 2 * C), 1)
    same_head = (ri >> 6) == (ci >> 6)
    eye = jnp.where(ri == ci, 1.0, 0.0).astype(F32)
    same8 = (ri >> 3) == (ci >> 3)
    lvl = [((ri >> (s + 1)) == (ci >> (s + 1))) & ((ri >> s) != (ci >> s)) for s in (3, 4, 5)]

    def split(x):
        return jnp.concatenate([jnp.where(m_a, x, 0.0), jnp.where(m_a, 0.0, x)], axis=0)

    def bf(x):
        return x.astype(BF16)

    def mm(a, b):
        return jnp.dot(a, b, preferred_element_type=F32)

    def mm_nt(a, b):
        return lax.dot_general(a, b, (((1,), (1,)), ((), ())), preferred_element_type=F32)

    def mm_tn(a, b):
        return lax.dot_general(a, b, (((0,), (0,)), ((), ())), preferred_element_type=F32)

    def cumsum(lw):
        hi = bf(lw)
        r1 = lw - hi.astype(F32)
        mid = bf(r1)
        lo = bf(r1 - mid.astype(F32))
        return mm(tri, hi) + (mm(tri, mid) + mm(tri, lo))

    P = range(pg)
    sls = [slice(p * 2 * C, (p + 1) * 2 * C) for p in P]
    r = [r_ref[:, sl] for sl in sls]
    v = [v_ref[:, sl] for sl in sls]
    kk = [kk_ref[:, sl] for sl in sls]
    lw = [lw_ref[0, :, sl] for sl in sls]
    kd = [kd_ref[0, :, sl] for sl in sls]
    b = [b_ref[0, :, sl] for sl in sls]
    cl = [cumsum(lw[p]) for p in P]
    tot = [jnp.sum(lw[p], axis=0, keepdims=True) for p in P]
    e_neg = [jnp.exp(-cl[p]) for p in P]
    e_end = [jnp.exp(tot[p] - cl[p]) for p in P]
    x = [bf(jnp.concatenate([r[p] * jnp.exp(cl[p]), kk[p] * jnp.exp(cl[p] - lw[p])], axis=0)) for p in P]
    z = [bf(jnp.concatenate([split(kd[p] * e_neg[p]), split(b[p] * e_neg[p])], axis=0)) for p in P]
    pm = [mm_nt(x[p], z[p]) for p in P]
    ld = [split(jnp.where(strict, pm[p][C:2 * C, 2 * C:4 * C], 0.0)) for p in P]
    l8 = [jnp.where(same8, ld[p], 0.0) for p in P]
    l8b = [bf(l8[p]) for p in P]
    l8_2 = [mm(l8b[p], l8b[p]) for p in P]
    l8_2b = [bf(l8_2[p]) for p in P]
    l8_4 = [mm(l8_2b[p], l8_2b[p]) for p in P]
    t1 = [mm(bf(eye - l8[p]), bf(eye + l8_2[p])) for p in P]
    inv = [mm(bf(t1[p]), bf(eye + l8_4[p])) for p in P]
    for msk in lvl:
        invb = [bf(inv[p]) for p in P]
        t2 = [mm(invb[p], bf(jnp.where(msk, ld[p], 0.0))) for p in P]
        inv = [inv[p] - mm(bf(t2[p]), invb[p]) for p in P]
    invb = [bf(inv[p]) for p in P]
    st = [s_ref[p] for p in P]
    xs = [mm_nt(x[p], bf(st[p])) for p in P]
    vs = [bf(split(v[p])) for p in P]
    rhs = [xs[p][C:2 * C] + mm(bf(jnp.where(strict, pm[p][C:2 * C, 0:2 * C], 0.0)), vs[p]) for p in P]
    ud = [mm(invb[p], bf(split(rhs[p]))) for p in P]
    for p in P:
        coef = jnp.concatenate([jnp.where(incl, pm[p][0:C, 0:2 * C], 0.0),
                                jnp.where(incl, -pm[p][0:C, 2 * C:4 * C], 0.0)], axis=1)
        y_ref[0, :, sls[p]] = xs[p][0:C] + mm(bf(coef), jnp.concatenate([vs[p], bf(ud[p])], axis=0))
    for p in P:
        u = ud[p][0:C] + ud[p][C:2 * C]
        upd = mm_tn(bf(jnp.concatenate([v[p], u], axis=0)),
                    bf(jnp.concatenate([kd[p] * e_end[p], -(b[p] * e_end[p])], axis=0)))
        s_ref[p] = st[p] * jnp.exp(tot[p]) + jnp.where(same_head, upd, 0.0)


def rwkv_scan(r, v, kk, lw, kd, b, pg=16):
    n = r.shape[0]
    nc = n // CHUNK
    w = pg * 2 * CHUNK

    def cmap(d, g, c):
        return (c + d * (nc - 1 - 2 * c), g)

    def dmap(d, g, c):
        return (d, c + d * (nc - 1 - 2 * c), g)

    one = pl.BlockSpec((CHUNK, w), cmap)
    two = pl.BlockSpec((1, CHUNK, w), dmap)
    return pl.pallas_call(
        functools.partial(_scan_kernel, pg=pg, nc=nc),
        grid=(2, RW // w, nc),
        in_specs=[one, one, one, two, two, two],
        out_specs=two,
        out_shape=jax.ShapeDtypeStruct((2, n, RW), F32),
        scratch_shapes=[pltpu.VMEM((pg, 2 * CHUNK, 2 * CHUNK), F32)],
        compiler_params=_cparams(("parallel", "parallel", "arbitrary")),
        name="rwkv_scan",
    )(r, v, kk, lw, kd, b)


def _rwkv_post_kernel(y_ref, bonus_ref, g_ref, w_ref, b_ref, bd_ref, o_ref):
    bd = bd_ref[...]

    def segmean(z):
        return _head_sum(z, bd) * (1.0 / RK)

    y = y_ref[0] + y_ref[1]
    yc = y - segmean(y)
    var = segmean(yc * yc)
    out = yc * lax.rsqrt(var + LNX_EPS) * w_ref[...] + b_ref[...]
    o_ref[...] = ((out + bonus_ref[...]) * g_ref[...]).astype(o_ref.dtype)


def rwkv_post(y, bonus, g, lnx_w, lnx_b, tm=256, tc=512):
    n = bonus.shape[0]
    one = pl.BlockSpec((tm, tc), lambda i, j: (i, j))
    ch = pl.BlockSpec((1, tc), lambda i, j: (0, j))
    return pl.pallas_call(
        _rwkv_post_kernel,
        grid=(n // tm, RW // tc),
        in_specs=[pl.BlockSpec((2, tm, tc), lambda i, j: (0, i, j)), one, one, ch, ch,
                  pl.BlockSpec((tc, tc), lambda i, j: (0, 0))],
        out_specs=one,
        out_shape=jax.ShapeDtypeStruct((n, RW), BF16),
        compiler_params=_cparams(("parallel", "parallel")),
        name="rwkv_post",
    )(y, bonus, g, lnx_w, lnx_b, _head_block_ones(tc))


HALO = 16


FF_TN = 256
FF_TILES = D_FF // FF_TN


def _ffn_up_kernel(h_ref, hp_ref, hn_ref, wv_ref, wg_ref, cv, cg, bv, bg, o_ref, wvb_ref, wgb_ref, *, tm):
    j = pl.program_id(0)

    @pl.when(j < FF_TILES)
    def _():
        _cast_weight_once(wv_ref, wvb_ref)
        _cast_weight_once(wg_ref, wgb_ref)
        row0 = pl.program_id(1) * tm
        h = h_ref[...]
        edge = jnp.concatenate([hp_ref[...], hn_ref[...]], axis=0)

        def branch(wb_ref, c_ref, b_ref):
            w = wb_ref[...]
            u = jnp.dot(h, w, preferred_element_type=F32)
            ue = jnp.dot(edge, w, preferred_element_type=F32)
            up, un = _neighbours(u, ue[HALO - 1:HALO, :], ue[HALO:HALO + 1, :], row0)
            return up * c_ref[0:1, :] + u * c_ref[1:2, :] + un * c_ref[2:3, :] + b_ref[...]

        gate = branch(wgb_ref, cg, bg)
        act = gate * jax.nn.sigmoid(gate)
        o_ref[...] = (act * branch(wvb_ref, cv, bv)).astype(o_ref.dtype)

    @pl.when(j == FF_TILES)
    def _():
        o_ref[...] = jnp.zeros_like(o_ref)


def ffn_up_act(h, w_up, conv_w, conv_b, layer, tm=1024):
    n, k = h.shape
    tn = FF_TN
    nhb = n // HALO
    tb = tm // HALO

    def col(j):
        return jnp.minimum(j, FF_TILES - 1)

    def chan(rows, o):
        return pl.BlockSpec((None, rows, tn), lambda j, i: (layer, 0, o + col(j)))

    return pl.pallas_call(
        functools.partial(_ffn_up_kernel, tm=tm),
        grid=(D_FF_PAD // tn, n // tm),
        in_specs=[
            pl.BlockSpec((tm, k), lambda j, i: (i, 0)),
            pl.BlockSpec((HALO, k), lambda j, i: (jnp.maximum(i * tb - 1, 0), 0)),
            pl.BlockSpec((HALO, k), lambda j, i: (jnp.minimum((i + 1) * tb, nhb - 1), 0)),
            chan(k, 0), chan(k, FF_TILES),
            chan(3, 0), chan(3, FF_TILES), chan(1, 0), chan(1, FF_TILES),
        ],
        out_specs=pl.BlockSpec((tm, tn), lambda j, i: (i, j)),
        out_shape=jax.ShapeDtypeStruct((n, D_FF_PAD), BF16),
        scratch_shapes=[pltpu.VMEM((k, tn), BF16), pltpu.VMEM((k, tn), BF16)],
        compiler_params=_cparams(("parallel", "arbitrary")),
        name="ffn_up_act",
    )(h, h, h, w_up, w_up, conv_w, conv_w, conv_b, conv_b)


def _layer_params(l, norm_mix, q_gain, k_gain, rwkv_conv, decay_w0, decay_w2, iclr_a0, iclr_a2,
                  gate_g2, k_k, k_a, r_k, lnx_w, lnx_b, norm_ffn, w_ffn_down):
    gains = jnp.stack([q_gain[l]] * 4 + [k_gain[l]] * 2)[:, None, :]
    return dict(
        norm_mix=norm_mix[l], gains=gains, rwkv_conv=rwkv_conv[l],
        decay_w0=decay_w0[l], decay_w2=decay_w2[l].astype(BF16),
        iclr_a0=iclr_a0[l], iclr_a2=iclr_a2[l].astype(BF16),
        gate_g2=jnp.pad(gate_g2[l], ((0, GATE_LORA_PAD - GATE_LORA), (0, 0))).astype(BF16),
        k_k=k_k[l][None, :], k_a=k_a[l][None, :], r_k=r_k[l].reshape(1, RW),
        lnx_w=lnx_w[l][None, :], lnx_b=lnx_b[l][None, :], norm_ffn=norm_ffn[l],
        w_ffn_down=jnp.pad(w_ffn_down[l], ((0, D_FF_PAD - D_FF), (0, 0))).astype(BF16),
    )


def _layer(x, l, p, big, tables):
    h = rmsnorm_cast(x, p["norm_mix"])
    proj = matmul(h, big["w_in"], l, C_GA, name="in_proj")
    gates = matmul(h, big["w_gates"], l, 2 * D_MODEL, name="gate_proj")
    qkv = qkv_prep(proj, p["gains"], tables)
    att = jnp.concatenate([attention(qkv, 0, P_BATCH, P_SEQ),
                           attention(qkv, N_PROMPT, S_BATCH, S_SEQ)], axis=0)
    r, v, kk, bonus, g, lw, kd, b = rwkv_prep(proj, p)
    y = rwkv_scan(r, v, kk, lw, kd, b)
    rw = rwkv_post(y, bonus, g, p["lnx_w"], p["lnx_b"])
    mixed = merge_branches(att, rw, big["w_up_attn"], big["w_up_rwkv"], l, gates)
    x = matmul(mixed, big["w_o"], l, D_MODEL, res=x, name="out_proj")
    h = rmsnorm_cast(x, p["norm_ffn"])
    act = ffn_up_act(h, big["w_ffn_up"], big["ffn_conv"], big["ffn_conv_b"], l)
    return matmul_ktiled_res(act, p["w_ffn_down"], x, name="ffn_down")


def kernel(x_prompt, x_sample, norm_mix, w_in, q_gain, k_gain, rwkv_conv, decay_w0, decay_w2, iclr_a0, iclr_a2, gate_g2, k_k, k_a, r_k, lnx_w, lnx_b, w_up_attn, w_up_rwkv, w_o, norm_ffn, w_ffn_up, ffn_conv, ffn_conv_b, w_ffn_down):
    small = (norm_mix, q_gain, k_gain, rwkv_conv, decay_w0, decay_w2, iclr_a0, iclr_a2,
             gate_g2, k_k, k_a, r_k, lnx_w, lnx_b, norm_ffn, w_ffn_down)
    big = dict(w_in=w_in, w_gates=w_in[:, :, IN_GATES:], w_up_attn=w_up_attn, w_up_rwkv=w_up_rwkv, w_o=w_o,
               w_ffn_up=w_ffn_up, ffn_conv=ffn_conv, ffn_conv_b=ffn_conv_b[:, None, :])
    x = jnp.concatenate([x_prompt.reshape(N_PROMPT, D_MODEL), x_sample.reshape(-1, D_MODEL)], axis=0)
    tables = _rope_tables()
    for l in range(norm_mix.shape[0]):
        x = _layer(x, l, _layer_params(l, *small), big, tables)
    return (x[:N_PROMPT].reshape(P_BATCH, P_SEQ, D_MODEL), x[N_PROMPT:].reshape(S_BATCH, S_SEQ, D_MODEL))
```

```python
import functools

import jax
import jax.numpy as jnp
import numpy as np
from jax import lax
from jax.experimental import pallas as pl
from jax.experimental.pallas import tpu as pltpu

F32 = jnp.float32
BF16 = jnp.bfloat16

D_MODEL = 4096
P_BATCH, P_SEQ = 4, 2048
S_BATCH, S_SEQ = 2, 4096
N_PROMPT = P_BATCH * P_SEQ
N_TOK = N_PROMPT + S_BATCH * S_SEQ
GRID_W = 64
HEAD = 128
ATT_W = 2048
KV_W = 512
KV_HEADS = 4
GROUPS = 4
ROPE_THETA = 10000.0
RK = 64
RW = 2048
LORA = 128
GATE_LORA = 480
GATE_LORA_PAD = 512
D_FF = 11008
D_FF_PAD = 11264
NORM_EPS = 1e-6
LNX_EPS = RK * 1e-5

C_Q, C_K, C_V = 0, 2048, 2560
C_R, C_RK, C_RV = 3072, 5120, 7168
C_WL, C_AL, C_GL = 9216, 9472, 9728
C_GA = C_GL + GATE_LORA_PAD
IN_GATES = C_GL + GATE_LORA

CHUNK = 64
VMEM_LIMIT = 56 * 1024 * 1024


def _cparams(sem):
    return pltpu.CompilerParams(dimension_semantics=sem, vmem_limit_bytes=VMEM_LIMIT)


def _seq_edges(rows):
    pos = rows & jnp.where(rows < N_PROMPT, P_SEQ - 1, S_SEQ - 1)
    last = jnp.where(rows < N_PROMPT, P_SEQ - 1, S_SEQ - 1)
    return pos == 0, pos == last


def _neighbours(x, prev_row, next_row, row0):
    tm = x.shape[0]
    assert P_SEQ % tm == 0 and S_SEQ % tm == 0
    li = lax.broadcasted_iota(jnp.int32, x.shape, 0)
    starts, _ = _seq_edges(row0)
    _, ends = _seq_edges(row0 + (tm - 1))
    prev_row = jnp.where(starts, 0.0, prev_row)
    next_row = jnp.where(ends, 0.0, next_row)
    xp = jnp.where(li == 0, prev_row, pltpu.roll(x, 1, axis=0))
    xn = jnp.where(li == tm - 1, next_row, pltpu.roll(x, tm - 1, axis=0))
    return xp, xn


def _rmsnorm_kernel(x_ref, g_ref, o_ref):
    x = x_ref[...]
    ms = jnp.mean(x * x, axis=-1, keepdims=True)
    o_ref[...] = (x * lax.rsqrt(ms + NORM_EPS) * g_ref[...]).astype(o_ref.dtype)


def rmsnorm_cast(x, g, tm=256):
    n, d = x.shape
    return pl.pallas_call(
        _rmsnorm_kernel,
        grid=(n // tm,),
        in_specs=[pl.BlockSpec((tm, d), lambda i: (i, 0)), pl.BlockSpec((1, d), lambda i: (0, 0))],
        out_specs=pl.BlockSpec((tm, d), lambda i: (i, 0)),
        out_shape=jax.ShapeDtypeStruct((n, d), BF16),
        compiler_params=_cparams(("parallel",)),
        name="rmsnorm_cast",
    )(x, g.reshape(1, d))


def _cast_weight_once(w_ref, wb_ref):
    @pl.when(pl.program_id(1) == 0)
    def _():
        wb_ref[...] = w_ref[...].astype(BF16)


def _mm_kernel(a_ref, b_ref, o_ref, bb_ref):
    _cast_weight_once(b_ref, bb_ref)
    o_ref[...] = jnp.dot(a_ref[...], bb_ref[...], preferred_element_type=F32).astype(o_ref.dtype)


def _mm_res_kernel(a_ref, b_ref, r_ref, o_ref, bb_ref):
    _cast_weight_once(b_ref, bb_ref)
    o_ref[...] = r_ref[...] + jnp.dot(a_ref[...], bb_ref[...], preferred_element_type=F32)


def matmul(a, w, layer, n_cols, res=None, *, col0=0, tm=512, tn=512, name="matmul"):
    m, k = a.shape
    cb0 = col0 // tn
    in_specs = [pl.BlockSpec((tm, k), lambda j, i: (i, 0)),
                pl.BlockSpec((None, k, tn), lambda j, i: (layer, 0, cb0 + j))]
    args = [a, w]
    kern = _mm_kernel
    if res is not None:
        in_specs.append(pl.BlockSpec((tm, tn), lambda j, i: (i, j)))
        args.append(res)
        kern = _mm_res_kernel
    return pl.pallas_call(
        kern,
        grid=(n_cols // tn, m // tm),
        in_specs=in_specs,
        out_specs=pl.BlockSpec((tm, tn), lambda j, i: (i, j)),
        out_shape=jax.ShapeDtypeStruct((m, n_cols), F32),
        scratch_shapes=[pltpu.VMEM((k, tn), BF16)],
        compiler_params=_cparams(("parallel", "arbitrary")),
        name=name,
    )(*args)


def _gate_proj_kernel(a_ref, wl_ref, wr_ref, o_ref, wb_ref, *, shift):
    @pl.when(pl.program_id(1) == 0)
    def _():
        wb_ref[...] = jnp.concatenate([wl_ref[:, shift:], wr_ref[:, :shift]], axis=1).astype(BF16)

    o_ref[...] = jnp.dot(a_ref[...], wb_ref[...], preferred_element_type=F32)


def gate_proj(a, w_in, layer, tm=1024, tn=256):
    m, k = a.shape
    cb0, shift = IN_GATES // tn, IN_GATES % tn
    return pl.pallas_call(
        functools.partial(_gate_proj_kernel, shift=shift),
        grid=(2 * D_MODEL // tn, m // tm),
        in_specs=[pl.BlockSpec((tm, k), lambda j, i: (i, 0)),
                  pl.BlockSpec((None, k, tn), lambda j, i: (layer, 0, cb0 + j)),
                  pl.BlockSpec((None, k, tn), lambda j, i: (layer, 0, cb0 + j + 1))],
        out_specs=pl.BlockSpec((tm, tn), lambda j, i: (i, j)),
        out_shape=jax.ShapeDtypeStruct((m, 2 * D_MODEL), F32),
        scratch_shapes=[pltpu.VMEM((k, tn), BF16)],
        compiler_params=_cparams(("parallel", "arbitrary")),
        name="gate_proj",
    )(a, w_in, w_in)


def _mmk_res_kernel(a_ref, b_ref, r_ref, o_ref, acc_ref):
    kk = pl.program_id(2)

    @pl.when(kk == 0)
    def _():
        acc_ref[...] = r_ref[...]

    acc_ref[...] += jnp.dot(a_ref[...], b_ref[...], preferred_element_type=F32)

    @pl.when(kk == pl.num_programs(2) - 1)
    def _():
        o_ref[...] = acc_ref[...]


def matmul_ktiled_res(a, b, res, *, tm=1024, tn=1024, tk=1408, name="matmul_k"):
    m, k = a.shape
    n = b.shape[1]
    return pl.pallas_call(
        _mmk_res_kernel,
        grid=(m // tm, n // tn, k // tk),
        in_specs=[
            pl.BlockSpec((tm, tk), lambda i, j, kk: (i, kk)),
            pl.BlockSpec((tk, tn), lambda i, j, kk: (kk, j)),
            pl.BlockSpec((tm, tn), lambda i, j, kk: (i, j)),
        ],
        out_specs=pl.BlockSpec((tm, tn), lambda i, j, kk: (i, j)),
        out_shape=jax.ShapeDtypeStruct((m, n), F32),
        scratch_shapes=[pltpu.VMEM((tm, tn), F32)],
        compiler_params=_cparams(("parallel", "parallel", "arbitrary")),
        name=name,
    )(a, b, res)


def _merge_kernel(att_ref, rw_ref, wa_ref, wr_ref, ga_ref, gr_ref, o_ref, wab_ref, wrb_ref):
    _cast_weight_once(wa_ref, wab_ref)
    _cast_weight_once(wr_ref, wrb_ref)
    ya = jnp.dot(att_ref[...], wab_ref[...], preferred_element_type=F32)
    yr = jnp.dot(rw_ref[...], wrb_ref[...], preferred_element_type=F32)
    o_ref[...] = (jax.nn.sigmoid(ga_ref[...]) * ya + jax.nn.sigmoid(gr_ref[...]) * yr).astype(o_ref.dtype)


def merge_branches(att, rw, wa, wr, layer, gates, tm=512, tn=512):
    n = att.shape[0]
    gr0 = D_MODEL // tn
    return pl.pallas_call(
        _merge_kernel,
        grid=(D_MODEL // tn, n // tm),
        in_specs=[
            pl.BlockSpec((tm, ATT_W), lambda j, i: (i, 0)),
            pl.BlockSpec((tm, RW), lambda j, i: (i, 0)),
            pl.BlockSpec((None, ATT_W, tn), lambda j, i: (layer, 0, j)),
            pl.BlockSpec((None, RW, tn), lambda j, i: (layer, 0, j)),
            pl.BlockSpec((tm, tn), lambda j, i: (i, j)),
            pl.BlockSpec((tm, tn), lambda j, i: (i, gr0 + j)),
        ],
        out_specs=pl.BlockSpec((tm, tn), lambda j, i: (i, j)),
        out_shape=jax.ShapeDtypeStruct((n, D_MODEL), BF16),
        scratch_shapes=[pltpu.VMEM((ATT_W, tn), BF16), pltpu.VMEM((RW, tn), BF16)],
        compiler_params=_cparams(("parallel", "arbitrary")),
        name="merge_branches",
    )(att, rw, wa, wr, gates, gates)


def _qkv_proj_kernel(a_ref, w_ref, gain_ref, cos_ref, sa_ref, sb_ref, o_ref, wb_ref):
    _cast_weight_once(w_ref, wb_ref)
    acc = jnp.dot(a_ref[...], wb_ref[...], preferred_element_type=F32)
    j = pl.program_id(0)

    @pl.when(j < 5)
    def _():
        cos, sa, sb = cos_ref[...], sa_ref[...], sb_ref[...]
        gain = gain_ref[0]
        for h in range(4):
            x = acc[:, h * HEAD:(h + 1) * HEAD]
            ms = jnp.mean(x * x, axis=-1, keepdims=True)
            y = x * lax.rsqrt(ms + NORM_EPS) * gain
            y = y * cos + pltpu.roll(y, 96, axis=1) * sa + pltpu.roll(y, 32, axis=1) * sb
            o_ref[:, h * HEAD:(h + 1) * HEAD] = y.astype(o_ref.dtype)

    @pl.when(j == 5)
    def _():
        o_ref[...] = acc.astype(o_ref.dtype)


def _rope_tables():
    t = np.arange(S_SEQ)
    row = (t // GRID_W).astype(np.float32)
    col = (t % GRID_W).astype(np.float32)
    axis_dim = HEAD // 2
    inv = jnp.asarray(ROPE_THETA, F32) ** (-jnp.arange(0, axis_dim, 2, dtype=F32) / axis_dim)
    ang_r = jnp.asarray(row)[:, None] * inv
    ang_c = jnp.asarray(col)[:, None] * inv
    cr, sr, cc, sc = jnp.cos(ang_r), jnp.sin(ang_r), jnp.cos(ang_c), jnp.sin(ang_c)
    z = jnp.zeros_like(sr)
    cos = jnp.concatenate([cr, cr, cc, cc], axis=1)
    sin_a = jnp.concatenate([-sr, z, -sc, z], axis=1)
    sin_b = jnp.concatenate([z, sr, z, sc], axis=1)
    return cos, sin_a, sin_b


def qkv_proj(h, w_in, layer, gains, tables, tm=512, tn=512):
    n, k = h.shape
    npb = N_PROMPT // tm

    def tab_map(j, i):
        return (jnp.where(i < npb, i % (P_SEQ // tm), (i - npb) % (S_SEQ // tm)), 0)

    tab_spec = pl.BlockSpec((tm, HEAD), tab_map)
    return pl.pallas_call(
        _qkv_proj_kernel,
        grid=(C_R // tn, n // tm),
        in_specs=[
            pl.BlockSpec((tm, k), lambda j, i: (i, 0)),
            pl.BlockSpec((None, k, tn), lambda j, i: (layer, 0, j)),
            pl.BlockSpec((1, 1, HEAD), lambda j, i: (j, 0, 0)),
            tab_spec, tab_spec, tab_spec,
        ],
        out_specs=pl.BlockSpec((tm, tn), lambda j, i: (i, j)),
        out_shape=jax.ShapeDtypeStruct((n, C_R), BF16),
        scratch_shapes=[pltpu.VMEM((k, tn), BF16)],
        compiler_params=_cparams(("parallel", "arbitrary")),
        name="qkv_proj",
    )(h, w_in, gains, *tables)


def _attn_kernel(q_ref, k_ref, v_ref, o_ref):
    k = k_ref[...]
    v = v_ref[...]
    for g in range(GROUPS):
        q = q_ref[:, g * HEAD:(g + 1) * HEAD]
        s = lax.dot_general(q, k, (((1,), (1,)), ((), ())), preferred_element_type=F32)
        m = jnp.max(s, axis=-1, keepdims=True)
        p = jnp.exp(s - m)
        l = jnp.sum(p, axis=-1, keepdims=True)
        o = jnp.dot(p.astype(BF16), v, preferred_element_type=F32)
        o_ref[:, g * HEAD:(g + 1) * HEAD] = (o / l).astype(o_ref.dtype)


def attention(qkv, row0, batch, seq, tq=256):
    qb0 = row0 // tq
    sb0 = row0 // seq
    nq = seq // tq
    return pl.pallas_call(
        _attn_kernel,
        grid=(batch, KV_HEADS, nq),
        in_specs=[
            pl.BlockSpec((tq, GROUPS * HEAD), lambda b, h, i: (qb0 + b * nq + i, h)),
            pl.BlockSpec((seq, HEAD), lambda b, h, i: (sb0 + b, C_K // HEAD + h)),
            pl.BlockSpec((seq, HEAD), lambda b, h, i: (sb0 + b, C_V // HEAD + h)),
        ],
        out_specs=pl.BlockSpec((tq, GROUPS * HEAD), lambda b, h, i: (b * nq + i, h)),
        out_shape=jax.ShapeDtypeStruct((batch * seq, ATT_W), BF16),
        compiler_params=_cparams(("parallel", "parallel", "arbitrary")),
        name="attention",
    )(qkv, qkv, qkv)


def _rwkv_prep_kernel(xr, xr_p, xr_n, xk, xk_p, xk_n, xv, xv_p, xv_n, wl_ref, al_ref, gl_ref,
                      cr, ck, cv, w0, w2, a0, a2, g2, kkw, kaw, rkw, bd_ref,
                      r_o, v_o, kk_o, bonus_o, g_o, lw_o, kd_o, b_o, *, tm):
    row0 = pl.program_id(0) * tm

    def conv(x_ref, p_ref, n_ref, c_ref):
        x = x_ref[...]
        xp, xn = _neighbours(x, p_ref[7:8, :], n_ref[0:1, :], row0)
        return xp * c_ref[0:1, :] + x * c_ref[1:2, :] + xn * c_ref[2:3, :]

    r = conv(xr, xr_p, xr_n, cr)
    k = conv(xk, xk_p, xk_n, ck)
    v = conv(xv, xv_p, xv_n, cv)
    bd = bd_ref[...]

    def segsum(z):
        return _head_sum(z, bd)

    kkr = k * kkw[...]
    kk = kkr / jnp.maximum(jnp.sqrt(segsum(kkr * kkr)), 1e-12)
    g = jnp.dot(jax.nn.sigmoid(gl_ref[...]).astype(BF16), g2[...], preferred_element_type=F32)
    r_o[...] = r
    v_o[...] = v.astype(v_o.dtype)
    kk_o[...] = kk
    g_o[...] = g
    bc = jnp.zeros_like(r)
    for d in range(2):
        wl = jnp.tanh(wl_ref[:, d * LORA:(d + 1) * LORA]).astype(BF16)
        w_raw = w0[d:d + 1, :] + jnp.dot(wl, w2[d], preferred_element_type=F32)
        lw_o[d] = jax.nn.sigmoid(w_raw) * (-float(np.exp(-0.5)))
        al = al_ref[:, d * LORA:(d + 1) * LORA].astype(BF16)
        a = jax.nn.sigmoid(a0[d:d + 1, :] + jnp.dot(al, a2[d], preferred_element_type=F32))
        kd = k * (1.0 + (a - 1.0) * kaw[...])
        kd_o[d] = kd
        b_o[d] = a * kk
        bc = bc + r * kd * rkw[...]
    bonus_o[...] = segsum(bc) * v


def _head_block_ones(width):
    i = np.arange(width) // RK
    return jnp.asarray((i[:, None] == i[None, :]).astype(np.float32)).astype(BF16)


def _head_sum(z, ones_bd):
    hi = z.astype(BF16)
    lo = (z - hi.astype(F32)).astype(BF16)
    return (jnp.dot(hi, ones_bd, preferred_element_type=F32)
            + jnp.dot(lo, ones_bd, preferred_element_type=F32))


def rwkv_prep(proj, p, tm=256, tc=512):
    n = proj.shape[0]
    nrb = n // 8
    tb = tm // 8

    def main(c0):
        return pl.BlockSpec((tm, tc), lambda i, j: (i, c0 // tc + j))

    def prev(c0):
        return pl.BlockSpec((8, tc), lambda i, j: (jnp.maximum(i * tb - 1, 0), c0 // tc + j))

    def nxt(c0):
        return pl.BlockSpec((8, tc), lambda i, j: (jnp.minimum((i + 1) * tb, nrb - 1), c0 // tc + j))

    def chan(rows, c0=0):
        return pl.BlockSpec((rows, tc), lambda i, j: (0, c0 // tc + j))

    in_specs = []
    for c0 in (C_R, C_RK, C_RV):
        in_specs += [main(c0 - C_R), prev(c0 - C_R), nxt(c0 - C_R)]
    in_specs += [
        pl.BlockSpec((tm, 2 * LORA), lambda i, j: (i, (C_WL - C_R) // (2 * LORA))),
        pl.BlockSpec((tm, 2 * LORA), lambda i, j: (i, (C_AL - C_R) // (2 * LORA))),
        pl.BlockSpec((tm, GATE_LORA_PAD), lambda i, j: (i, (C_GL - C_R) // GATE_LORA_PAD)),
        chan(3, 0), chan(3, RW), chan(3, 2 * RW),
        chan(2),
        pl.BlockSpec((2, LORA, tc), lambda i, j: (0, 0, j)),
        chan(2),
        pl.BlockSpec((2, LORA, tc), lambda i, j: (0, 0, j)),
        pl.BlockSpec((GATE_LORA_PAD, tc), lambda i, j: (0, j)),
        chan(1), chan(1), chan(1),
        pl.BlockSpec((tc, tc), lambda i, j: (0, 0)),
    ]
    one = pl.BlockSpec((tm, tc), lambda i, j: (i, j))
    two = pl.BlockSpec((2, tm, tc), lambda i, j: (0, i, j))
    s1 = jax.ShapeDtypeStruct((n, RW), F32)
    s2 = jax.ShapeDtypeStruct((2, n, RW), F32)
    return pl.pallas_call(
        functools.partial(_rwkv_prep_kernel, tm=tm),
        grid=(n // tm, RW // tc),
        in_specs=in_specs,
        out_specs=[one, one, one, one, one, two, two, two],
        out_shape=[s1, jax.ShapeDtypeStruct((n, RW), BF16), s1, s1, s1, s2, s2, s2],
        compiler_params=_cparams(("parallel", "parallel")),
        name="rwkv_prep",
    )(proj, proj, proj, proj, proj, proj, proj, proj, proj, proj, proj, proj,
      p["rwkv_conv"], p["rwkv_conv"], p["rwkv_conv"], p["decay_w0"], p["decay_w2"], p["iclr_a0"],
      p["iclr_a2"], p["gate_g2"], p["k_k"], p["k_a"], p["r_k"], _head_block_ones(tc))


def _scan_kernel(r_ref, v_ref, kk_ref, lw_ref, kd_ref, b_ref, y_ref, s_ref, *, pg, nc):
    d = pl.program_id(0)
    c = pl.program_id(2)
    chunk = c + d * (nc - 1 - 2 * c)
    npc = N_PROMPT // CHUNK
    in_p = chunk < npc
    per = jnp.where(in_p, P_SEQ // CHUNK, S_SEQ // CHUNK)
    rel = jnp.where(in_p, chunk, chunk - npc)
    @pl.when((rel + d) % per == 0)
    def _():
        s_ref[...] = jnp.zeros_like(s_ref)

    sgn = 1 - 2 * d
    C = CHUNK
    t_i = lax.broadcasted_iota(jnp.int32, (C, 2 * C), 0)
    s_i = lax.broadcasted_iota(jnp.int32, (C, 2 * C), 1) & (C - 1)
    rel_ts = (t_i - s_i) * sgn
    strict = rel_ts > 0
    incl = rel_ts >= 0
    lane = lax.broadcasted_iota(jnp.int32, (C, 2 * C), 1)
    m_a = lane < C
    tt = lax.broadcasted_iota(jnp.int32, (C, C), 0)
    ss = lax.broadcasted_iota(jnp.int32, (C, C), 1)
    tri = jnp.where((tt - ss) * sgn >= 0, 1.0, 0.0).astype(BF16)
    ri = lax.broadcasted_iota(jnp.int32, (2 * C, 2 * C), 0)
    ci = lax.broadcasted_iota(jnp.int32, (2 * C, 2 * C), 1)
    same_head = (ri >> 6) == (ci >> 6)
    eye = jnp.where(ri == ci, 1.0, 0.0).astype(F32)
    same8 = (ri >> 3) == (ci >> 3)
    lvl = [((ri >> (s + 1)) == (ci >> (s + 1))) & ((ri >> s) != (ci >> s)) for s in (3, 4, 5)]

    def split(x):
        return jnp.concatenate([jnp.where(m_a, x, 0.0), jnp.where(m_a, 0.0, x)], axis=0)

    def bf(x):
        return x.astype(BF16)

    def mm(a, b):
        return jnp.dot(a, b, preferred_element_type=F32)

    def mm_nt(a, b):
        return lax.dot_general(a, b, (((1,), (1,)), ((), ())), preferred_element_type=F32)

    def mm_tn(a, b):
        return lax.dot_general(a, b, (((0,), (0,)), ((), ())), preferred_element_type=F32)

    def cumsum(lw):
        hi = bf(lw)
        r1 = lw - hi.astype(F32)
        mid = bf(r1)
        lo = bf(r1 - mid.astype(F32))
        return mm(tri, hi) + (mm(tri, mid) + mm(tri, lo))

    P = range(pg)
    sls = [slice(p * 2 * C, (p + 1) * 2 * C) for p in P]
    r = [r_ref[:, sl] for sl in sls]
    v = [v_ref[:, sl] for sl in sls]
    kk = [kk_ref[:, sl] for sl in sls]
    lw = [lw_ref[0, :, sl] for sl in sls]
    kd = [kd_ref[0, :, sl] for sl in sls]
    b = [b_ref[0, :, sl] for sl in sls]
    cl = [cumsum(lw[p]) for p in P]
    tot = [jnp.sum(lw[p], axis=0, keepdims=True) for p in P]
    e_neg = [jnp.exp(-cl[p]) for p in P]
    e_end = [jnp.exp(tot[p] - cl[p]) for p in P]
    x = [bf(jnp.concatenate([r[p] * jnp.exp(cl[p]), kk[p] * jnp.exp(cl[p] - lw[p])], axis=0)) for p in P]
    z = [bf(jnp.concatenate([split(kd[p] * e_neg[p]), split(b[p] * e_neg[p])], axis=0)) for p in P]
    pm = [mm_nt(x[p], z[p]) for p in P]
    ld = [split(jnp.where(strict, pm[p][C:2 * C, 2 * C:4 * C], 0.0)) for p in P]
    l8 = [jnp.where(same8, ld[p], 0.0) for p in P]
    l8b = [bf(l8[p]) for p in P]
    l8_2 = [mm(l8b[p], l8b[p]) for p in P]
    l8_2b = [bf(l8_2[p]) for p in P]
    l8_4 = [mm(l8_2b[p], l8_2b[p]) for p in P]
    t1 = [mm(bf(eye - l8[p]), bf(eye + l8_2[p])) for p in P]
    inv = [mm(bf(t1[p]), bf(eye + l8_4[p])) for p in P]
    for msk in lvl:
        invb = [bf(inv[p]) for p in P]
        t2 = [mm(invb[p], bf(jnp.where(msk, ld[p], 0.0))) for p in P]
        inv = [inv[p] - mm(bf(t2[p]), invb[p]) for p in P]
    invb = [bf(inv[p]) for p in P]
    st = [s_ref[p] for p in P]
    xs = [mm_nt(x[p], bf(st[p])) for p in P]
    vs = [bf(split(v[p])) for p in P]
    rhs = [xs[p][C:2 * C] + mm(bf(jnp.where(strict, pm[p][C:2 * C, 0:2 * C], 0.0)), vs[p]) for p in P]
    ud = [mm(invb[p], bf(split(rhs[p]))) for p in P]
    for p in P:
        coef = jnp.concatenate([jnp.where(incl, pm[p][0:C, 0:2 * C], 0.0),
                                jnp.where(incl, -pm[p][0:C, 2 * C:4 * C], 0.0)], axis=1)
        y_ref[0, :, sls[p]] = xs[p][0:C] + mm(bf(coef), jnp.concatenate([vs[p], bf(ud[p])], axis=0))
    for p in P:
        u = ud[p][0:C] + ud[p][C:2 * C]
        upd = mm_tn(jnp.concatenate([v[p], bf(u)], axis=0),
                    bf(jnp.concatenate([kd[p] * e_end[p], -(b[p] * e_end[p])], axis=0)))
        s_ref[p] = st[p] * jnp.exp(tot[p]) + jnp.where(same_head, upd, 0.0)


def rwkv_scan(r, v, kk, lw, kd, b, pg=16):
    n = r.shape[0]
    nc = n // CHUNK
    w = pg * 2 * CHUNK

    def cmap(d, g, c):
        return (c + d * (nc - 1 - 2 * c), g)

    def dmap(d, g, c):
        return (d, c + d * (nc - 1 - 2 * c), g)

    one = pl.BlockSpec((CHUNK, w), cmap)
    two = pl.BlockSpec((1, CHUNK, w), dmap)
    return pl.pallas_call(
        functools.partial(_scan_kernel, pg=pg, nc=nc),
        grid=(2, RW // w, nc),
        in_specs=[one, one, one, two, two, two],
        out_specs=two,
        out_shape=jax.ShapeDtypeStruct((2, n, RW), F32),
        scratch_shapes=[pltpu.VMEM((pg, 2 * CHUNK, 2 * CHUNK), F32)],
        compiler_params=_cparams(("parallel", "parallel", "arbitrary")),
        name="rwkv_scan",
    )(r, v, kk, lw, kd, b)


def _rwkv_post_kernel(y_ref, bonus_ref, g_ref, w_ref, b_ref, bd_ref, o_ref):
    bd = bd_ref[...]

    def segmean(z):
        return _head_sum(z, bd) * (1.0 / RK)

    y = y_ref[0] + y_ref[1]
    yc = y - segmean(y)
    var = segmean(yc * yc)
    out = yc * lax.rsqrt(var + LNX_EPS) * w_ref[...] + b_ref[...]
    o_ref[...] = ((out + bonus_ref[...]) * g_ref[...]).astype(o_ref.dtype)


def rwkv_post(y, bonus, g, lnx_w, lnx_b, tm=256, tc=512):
    n = bonus.shape[0]
    one = pl.BlockSpec((tm, tc), lambda i, j: (i, j))
    ch = pl.BlockSpec((1, tc), lambda i, j: (0, j))
    return pl.pallas_call(
        _rwkv_post_kernel,
        grid=(n // tm, RW // tc),
        in_specs=[pl.BlockSpec((2, tm, tc), lambda i, j: (0, i, j)), one, one, ch, ch,
                  pl.BlockSpec((tc, tc), lambda i, j: (0, 0))],
        out_specs=one,
        out_shape=jax.ShapeDtypeStruct((n, RW), BF16),
        compiler_params=_cparams(("parallel", "parallel")),
        name="rwkv_post",
    )(y, bonus, g, lnx_w, lnx_b, _head_block_ones(tc))


HALO = 16


FF_TN = 256
FF_TILES = D_FF // FF_TN


def _ffn_up_kernel(h_ref, hp_ref, hn_ref, wv_ref, wg_ref, cv, cg, bv, bg, o_ref, wvb_ref, wgb_ref, *, tm):
    j = pl.program_id(0)

    @pl.when(j < FF_TILES)
    def _():
        _cast_weight_once(wv_ref, wvb_ref)
        _cast_weight_once(wg_ref, wgb_ref)
        row0 = pl.program_id(1) * tm
        h = h_ref[...]
        edge = jnp.concatenate([hp_ref[...], hn_ref[...]], axis=0)

        def branch(wb_ref, c_ref, b_ref):
            w = wb_ref[...]
            u = jnp.dot(h, w, preferred_element_type=F32)
            ue = jnp.dot(edge, w, preferred_element_type=F32)
            up, un = _neighbours(u, ue[HALO - 1:HALO, :], ue[HALO:HALO + 1, :], row0)
            return up * c_ref[0:1, :] + u * c_ref[1:2, :] + un * c_ref[2:3, :] + b_ref[...]

        gate = branch(wgb_ref, cg, bg)
        act = gate * jax.nn.sigmoid(gate)
        o_ref[...] = (act * branch(wvb_ref, cv, bv)).astype(o_ref.dtype)

    @pl.when(j == FF_TILES)
    def _():
        o_ref[...] = jnp.zeros_like(o_ref)


def ffn_up_act(h, w_up, conv_w, conv_b, layer, tm=1024):
    n, k = h.shape
    tn = FF_TN
    nhb = n // HALO
    tb = tm // HALO

    def col(j):
        return jnp.minimum(j, FF_TILES - 1)

    def chan(rows, o):
        return pl.BlockSpec((None, rows, tn), lambda j, i: (layer, 0, o + col(j)))

    return pl.pallas_call(
        functools.partial(_ffn_up_kernel, tm=tm),
        grid=(D_FF_PAD // tn, n // tm),
        in_specs=[
            pl.BlockSpec((tm, k), lambda j, i: (i, 0)),
            pl.BlockSpec((HALO, k), lambda j, i: (jnp.maximum(i * tb - 1, 0), 0)),
            pl.BlockSpec((HALO, k), lambda j, i: (jnp.minimum((i + 1) * tb, nhb - 1), 0)),
            chan(k, 0), chan(k, FF_TILES),
            chan(3, 0), chan(3, FF_TILES), chan(1, 0), chan(1, FF_TILES),
        ],
        out_specs=pl.BlockSpec((tm, tn), lambda j, i: (i, j)),
        out_shape=jax.ShapeDtypeStruct((n, D_FF_PAD), BF16),
        scratch_shapes=[pltpu.VMEM((k, tn), BF16), pltpu.VMEM((k, tn), BF16)],
        compiler_params=_cparams(("parallel", "arbitrary")),
        name="ffn_up_act",
    )(h, h, h, w_up, w_up, conv_w, conv_w, conv_b, conv_b)


def _layer_params(l, norm_mix, q_gain, k_gain, rwkv_conv, decay_w0, decay_w2, iclr_a0, iclr_a2,
                  gate_g2, k_k, k_a, r_k, lnx_w, lnx_b, norm_ffn, w_ffn_down):
    gains = jnp.stack([q_gain[l] * HEAD ** -0.5] * 4 + [k_gain[l]] * 2)[:, None, :]
    return dict(
        norm_mix=norm_mix[l], gains=gains, rwkv_conv=rwkv_conv[l],
        decay_w0=decay_w0[l], decay_w2=decay_w2[l].astype(BF16),
        iclr_a0=iclr_a0[l], iclr_a2=iclr_a2[l].astype(BF16),
        gate_g2=jnp.pad(gate_g2[l], ((0, GATE_LORA_PAD - GATE_LORA), (0, 0))).astype(BF16),
        k_k=k_k[l][None, :], k_a=k_a[l][None, :], r_k=r_k[l].reshape(1, RW),
        lnx_w=lnx_w[l][None, :], lnx_b=lnx_b[l][None, :], norm_ffn=norm_ffn[l],
        w_ffn_down=jnp.pad(w_ffn_down[l], ((0, D_FF_PAD - D_FF), (0, 0))).astype(BF16),
    )


def _layer(x, l, p, big, tables):
    h = rmsnorm_cast(x, p["norm_mix"])
    qkv = qkv_proj(h, big["w_in"], l, p["gains"], tables)
    proj = matmul(h, big["w_in"], l, C_GA - C_R, col0=C_R, name="in_proj")
    gates = gate_proj(h, big["w_in"], l)
    att = jnp.concatenate([attention(qkv, 0, P_BATCH, P_SEQ),
                           attention(qkv, N_PROMPT, S_BATCH, S_SEQ)], axis=0)
    r, v, kk, bonus, g, lw, kd, b = rwkv_prep(proj, p)
    y = rwkv_scan(r, v, kk, lw, kd, b)
    rw = rwkv_post(y, bonus, g, p["lnx_w"], p["lnx_b"])
    mixed = merge_branches(att, rw, big["w_up_attn"], big["w_up_rwkv"], l, gates)
    x = matmul(mixed, big["w_o"], l, D_MODEL, res=x, name="out_proj")
    h = rmsnorm_cast(x, p["norm_ffn"])
    act = ffn_up_act(h, big["w_ffn_up"], big["ffn_conv"], big["ffn_conv_b"], l)
    return matmul_ktiled_res(act, p["w_ffn_down"], x, name="ffn_down")


def kernel(x_prompt, x_sample, norm_mix, w_in, q_gain, k_gain, rwkv_conv, decay_w0, decay_w2, iclr_a0, iclr_a2, gate_g2, k_k, k_a, r_k, lnx_w, lnx_b, w_up_attn, w_up_rwkv, w_o, norm_ffn, w_ffn_up, ffn_conv, ffn_conv_b, w_ffn_down):
    small = (norm_mix, q_gain, k_gain, rwkv_conv, decay_w0, decay_w2, iclr_a0, iclr_a2,
             gate_g2, k_k, k_a, r_k, lnx_w, lnx_b, norm_ffn, w_ffn_down)
    big = dict(w_in=w_in, w_up_attn=w_up_attn, w_up_rwkv=w_up_rwkv, w_o=w_o,
               w_ffn_up=w_ffn_up, ffn_conv=ffn_conv, ffn_conv_b=ffn_conv_b[:, None, :])
    x = jnp.concatenate([x_prompt.reshape(N_PROMPT, D_MODEL), x_sample.reshape(-1, D_MODEL)], axis=0)
    tables = _rope_tables()
    for l in range(norm_mix.shape[0]):
        x = _layer(x, l, _layer_params(l, *small), big, tables)
    return (x[:N_PROMPT].reshape(P_BATCH, P_SEQ, D_MODEL), x[N_PROMPT:].reshape(S_BATCH, S_SEQ, D_MODEL))
```

```python
import functools

import jax
import jax.numpy as jnp
import numpy as np
from jax import lax
from jax.experimental import pallas as pl
from jax.experimental.pallas import tpu as pltpu

F32 = jnp.float32
BF16 = jnp.bfloat16

D_MODEL = 4096
P_BATCH, P_SEQ = 4, 2048
S_BATCH, S_SEQ = 2, 4096
N_PROMPT = P_BATCH * P_SEQ
N_TOK = N_PROMPT + S_BATCH * S_SEQ
GRID_W = 64
HEAD = 128
ATT_W = 2048
KV_W = 512
KV_HEADS = 4
GROUPS = 4
ROPE_THETA = 10000.0
RK = 64
RW = 2048
LORA = 128
GATE_LORA = 480
GATE_LORA_PAD = 512
D_FF = 11008
D_FF_PAD = 11264
NORM_EPS = 1e-6
LNX_EPS = RK * 1e-5

C_Q, C_K, C_V = 0, 2048, 2560
C_R, C_RK, C_RV = 3072, 5120, 7168
C_WL, C_AL, C_GL = 9216, 9472, 9728
C_GA = C_GL + GATE_LORA_PAD
IN_GATES = C_GL + GATE_LORA

CHUNK = 64
VMEM_LIMIT = 56 * 1024 * 1024


def _cparams(sem):
    return pltpu.CompilerParams(dimension_semantics=sem, vmem_limit_bytes=VMEM_LIMIT)


def _seq_edges(rows):
    pos = rows & jnp.where(rows < N_PROMPT, P_SEQ - 1, S_SEQ - 1)
    last = jnp.where(rows < N_PROMPT, P_SEQ - 1, S_SEQ - 1)
    return pos == 0, pos == last


def _neighbours(x, prev_row, next_row, row0):
    tm = x.shape[0]
    assert P_SEQ % tm == 0 and S_SEQ % tm == 0
    li = lax.broadcasted_iota(jnp.int32, x.shape, 0)
    starts, _ = _seq_edges(row0)
    _, ends = _seq_edges(row0 + (tm - 1))
    prev_row = jnp.where(starts, 0.0, prev_row)
    next_row = jnp.where(ends, 0.0, next_row)
    xp = jnp.where(li == 0, prev_row, pltpu.roll(x, 1, axis=0))
    xn = jnp.where(li == tm - 1, next_row, pltpu.roll(x, tm - 1, axis=0))
    return xp, xn


def _rmsnorm_kernel(x_ref, g_ref, o_ref):
    x = x_ref[...]
    ms = jnp.mean(x * x, axis=-1, keepdims=True)
    o_ref[...] = (x * lax.rsqrt(ms + NORM_EPS) * g_ref[...]).astype(o_ref.dtype)


def rmsnorm_cast(x, g, tm=256):
    n, d = x.shape
    return pl.pallas_call(
        _rmsnorm_kernel,
        grid=(n // tm,),
        in_specs=[pl.BlockSpec((tm, d), lambda i: (i, 0)), pl.BlockSpec((1, d), lambda i: (0, 0))],
        out_specs=pl.BlockSpec((tm, d), lambda i: (i, 0)),
        out_shape=jax.ShapeDtypeStruct((n, d), BF16),
        compiler_params=_cparams(("parallel",)),
        name="rmsnorm_cast",
    )(x, g.reshape(1, d))


def _cast_weight_once(w_ref, wb_ref):
    @pl.when(pl.program_id(1) == 0)
    def _():
        wb_ref[...] = w_ref[...].astype(BF16)


def _mm_kernel(a_ref, b_ref, o_ref, bb_ref):
    _cast_weight_once(b_ref, bb_ref)
    o_ref[...] = jnp.dot(a_ref[...], bb_ref[...], preferred_element_type=F32).astype(o_ref.dtype)


def _mm_res_kernel(a_ref, b_ref, r_ref, o_ref, bb_ref):
    _cast_weight_once(b_ref, bb_ref)
    o_ref[...] = r_ref[...] + jnp.dot(a_ref[...], bb_ref[...], preferred_element_type=F32)


def matmul(a, w, layer, n_cols, res=None, *, col0=0, tm=512, tn=512, name="matmul"):
    m, k = a.shape
    cb0 = col0 // tn
    in_specs = [pl.BlockSpec((tm, k), lambda j, i: (i, 0)),
                pl.BlockSpec((None, k, tn), lambda j, i: (layer, 0, cb0 + j))]
    args = [a, w]
    kern = _mm_kernel
    if res is not None:
        in_specs.append(pl.BlockSpec((tm, tn), lambda j, i: (i, j)))
        args.append(res)
        kern = _mm_res_kernel
    return pl.pallas_call(
        kern,
        grid=(n_cols // tn, m // tm),
        in_specs=in_specs,
        out_specs=pl.BlockSpec((tm, tn), lambda j, i: (i, j)),
        out_shape=jax.ShapeDtypeStruct((m, n_cols), F32),
        scratch_shapes=[pltpu.VMEM((k, tn), BF16)],
        compiler_params=_cparams(("parallel", "arbitrary")),
        name=name,
    )(*args)


def _mmk_res_kernel(a_ref, b_ref, r_ref, o_ref, acc_ref):
    kk = pl.program_id(2)

    @pl.when(kk == 0)
    def _():
        acc_ref[...] = r_ref[...]

    acc_ref[...] += jnp.dot(a_ref[...], b_ref[...], preferred_element_type=F32)

    @pl.when(kk == pl.num_programs(2) - 1)
    def _():
        o_ref[...] = acc_ref[...]


def matmul_ktiled_res(a, b, res, *, row0=0, rows=None, tm=1024, tn=1024, tk=1408, name="matmul_k"):
    k = a.shape[1]
    m = a.shape[0] if rows is None else rows
    n = b.shape[1]
    rb0 = row0 // tm
    return pl.pallas_call(
        _mmk_res_kernel,
        grid=(m // tm, n // tn, k // tk),
        in_specs=[
            pl.BlockSpec((tm, tk), lambda i, j, kk: (rb0 + i, kk)),
            pl.BlockSpec((tk, tn), lambda i, j, kk: (kk, j)),
            pl.BlockSpec((tm, tn), lambda i, j, kk: (rb0 + i, j)),
        ],
        out_specs=pl.BlockSpec((tm, tn), lambda i, j, kk: (i, j)),
        out_shape=jax.ShapeDtypeStruct((m, n), F32),
        scratch_shapes=[pltpu.VMEM((tm, tn), F32)],
        compiler_params=_cparams(("parallel", "parallel", "arbitrary")),
        name=name,
    )(a, b, res)


def _merge_kernel(att_ref, rw_ref, wa_ref, wr_ref, ga_ref, gr_ref, o_ref, wab_ref, wrb_ref):
    _cast_weight_once(wa_ref, wab_ref)
    _cast_weight_once(wr_ref, wrb_ref)
    ya = jnp.dot(att_ref[...], wab_ref[...], preferred_element_type=F32)
    yr = jnp.dot(rw_ref[...], wrb_ref[...], preferred_element_type=F32)
    o_ref[...] = (jax.nn.sigmoid(ga_ref[...]) * ya + jax.nn.sigmoid(gr_ref[...]) * yr).astype(o_ref.dtype)


def merge_branches(att, rw, wa, wr, layer, gates, tm=512, tn=512):
    n = att.shape[0]
    gr0 = D_MODEL // tn
    return pl.pallas_call(
        _merge_kernel,
        grid=(D_MODEL // tn, n // tm),
        in_specs=[
            pl.BlockSpec((tm, ATT_W), lambda j, i: (i, 0)),
            pl.BlockSpec((tm, RW), lambda j, i: (i, 0)),
            pl.BlockSpec((None, ATT_W, tn), lambda j, i: (layer, 0, j)),
            pl.BlockSpec((None, RW, tn), lambda j, i: (layer, 0, j)),
            pl.BlockSpec((tm, tn), lambda j, i: (i, j)),
            pl.BlockSpec((tm, tn), lambda j, i: (i, gr0 + j)),
        ],
        out_specs=pl.BlockSpec((tm, tn), lambda j, i: (i, j)),
        out_shape=jax.ShapeDtypeStruct((n, D_MODEL), BF16),
        scratch_shapes=[pltpu.VMEM((ATT_W, tn), BF16), pltpu.VMEM((RW, tn), BF16)],
        compiler_params=_cparams(("parallel", "arbitrary")),
        name="merge_branches",
    )(att, rw, wa, wr, gates, gates)


def _qkv_proj_kernel(a_ref, w_ref, gain_ref, cos_ref, sa_ref, sb_ref, o_ref, wb_ref):
    _cast_weight_once(w_ref, wb_ref)
    acc = jnp.dot(a_ref[...], wb_ref[...], preferred_element_type=F32)
    j = pl.program_id(0)

    @pl.when(j < 5)
    def _():
        cos, sa, sb = cos_ref[...], sa_ref[...], sb_ref[...]
        gain = gain_ref[0]
        for h in range(4):
            x = acc[:, h * HEAD:(h + 1) * HEAD]
            ms = jnp.mean(x * x, axis=-1, keepdims=True)
            y = x * lax.rsqrt(ms + NORM_EPS) * gain
            y = y * cos + pltpu.roll(y, 96, axis=1) * sa + pltpu.roll(y, 32, axis=1) * sb
            o_ref[:, h * HEAD:(h + 1) * HEAD] = y.astype(o_ref.dtype)

    @pl.when(j == 5)
    def _():
        o_ref[...] = acc.astype(o_ref.dtype)


def _rope_tables():
    t = np.arange(S_SEQ)
    row = (t // GRID_W).astype(np.float32)
    col = (t % GRID_W).astype(np.float32)
    axis_dim = HEAD // 2
    inv = jnp.asarray(ROPE_THETA, F32) ** (-jnp.arange(0, axis_dim, 2, dtype=F32) / axis_dim)
    ang_r = jnp.asarray(row)[:, None] * inv
    ang_c = jnp.asarray(col)[:, None] * inv
    cr, sr, cc, sc = jnp.cos(ang_r), jnp.sin(ang_r), jnp.cos(ang_c), jnp.sin(ang_c)
    z = jnp.zeros_like(sr)
    cos = jnp.concatenate([cr, cr, cc, cc], axis=1)
    sin_a = jnp.concatenate([-sr, z, -sc, z], axis=1)
    sin_b = jnp.concatenate([z, sr, z, sc], axis=1)
    return cos, sin_a, sin_b


def qkv_proj(h, w_in, layer, gains, tables, tm=512, tn=512):
    n, k = h.shape
    npb = N_PROMPT // tm

    def tab_map(j, i):
        return (jnp.where(i < npb, i % (P_SEQ // tm), (i - npb) % (S_SEQ // tm)), 0)

    tab_spec = pl.BlockSpec((tm, HEAD), tab_map)
    return pl.pallas_call(
        _qkv_proj_kernel,
        grid=(C_R // tn, n // tm),
        in_specs=[
            pl.BlockSpec((tm, k), lambda j, i: (i, 0)),
            pl.BlockSpec((None, k, tn), lambda j, i: (layer, 0, j)),
            pl.BlockSpec((1, 1, HEAD), lambda j, i: (j, 0, 0)),
            tab_spec, tab_spec, tab_spec,
        ],
        out_specs=pl.BlockSpec((tm, tn), lambda j, i: (i, j)),
        out_shape=jax.ShapeDtypeStruct((n, C_R), BF16),
        scratch_shapes=[pltpu.VMEM((k, tn), BF16)],
        compiler_params=_cparams(("parallel", "arbitrary")),
        name="qkv_proj",
    )(h, w_in, gains, *tables)


def _attn_kernel(q_ref, k_ref, v_ref, o_ref, v1_ref):
    @pl.when(pl.program_id(2) == 0)
    def _():
        v1_ref[:, 0:HEAD] = v_ref[...]
        v1_ref[:, HEAD:2 * HEAD] = jnp.ones(v_ref.shape, v1_ref.dtype)

    k = k_ref[...]
    v1 = v1_ref[...]
    for g in range(GROUPS):
        q = q_ref[:, g * HEAD:(g + 1) * HEAD]
        s = lax.dot_general(q, k, (((1,), (1,)), ((), ())), preferred_element_type=F32)
        p = jnp.exp2(s - jnp.max(s, axis=-1, keepdims=True))
        ol = jnp.dot(p.astype(BF16), v1, preferred_element_type=F32)
        o_ref[:, g * HEAD:(g + 1) * HEAD] = (ol[:, 0:HEAD] / ol[:, HEAD:2 * HEAD]).astype(o_ref.dtype)


def attention(qkv, row0, batch, seq, tq=256):
    qb0 = row0 // tq
    sb0 = row0 // seq
    nq = seq // tq
    return pl.pallas_call(
        _attn_kernel,
        grid=(batch, KV_HEADS, nq),
        in_specs=[
            pl.BlockSpec((tq, GROUPS * HEAD), lambda b, h, i: (qb0 + b * nq + i, h)),
            pl.BlockSpec((seq, HEAD), lambda b, h, i: (sb0 + b, C_K // HEAD + h)),
            pl.BlockSpec((seq, HEAD), lambda b, h, i: (sb0 + b, C_V // HEAD + h)),
        ],
        out_specs=pl.BlockSpec((tq, GROUPS * HEAD), lambda b, h, i: (b * nq + i, h)),
        out_shape=jax.ShapeDtypeStruct((batch * seq, ATT_W), BF16),
        scratch_shapes=[pltpu.VMEM((seq, 2 * HEAD), BF16)],
        compiler_params=_cparams(("parallel", "parallel", "arbitrary")),
        name="attention",
    )(qkv, qkv, qkv)


def _rwkv_prep_kernel(xr, xr_p, xr_n, xk, xk_p, xk_n, xv, xv_p, xv_n, wl_ref, al_ref, gl_ref,
                      cr, ck, cv, w0, w2, a0, a2, g2, kkw, kaw, rkw, bd_ref,
                      r_o, v_o, kk_o, bonus_o, g_o, lw_o, kd_o, b_o, *, tm):
    row0 = pl.program_id(0) * tm

    def conv(x_ref, p_ref, n_ref, c_ref):
        x = x_ref[...]
        xp, xn = _neighbours(x, p_ref[7:8, :], n_ref[0:1, :], row0)
        return xp * c_ref[0:1, :] + x * c_ref[1:2, :] + xn * c_ref[2:3, :]

    r = conv(xr, xr_p, xr_n, cr)
    k = conv(xk, xk_p, xk_n, ck)
    v = conv(xv, xv_p, xv_n, cv)
    bd = bd_ref[...]

    def segsum(z):
        return _head_sum(z, bd)

    kkr = k * kkw[...]
    kk = kkr / jnp.maximum(jnp.sqrt(segsum(kkr * kkr)), 1e-12)
    g = jnp.dot(jax.nn.sigmoid(gl_ref[...]).astype(BF16), g2[...], preferred_element_type=F32)
    r_o[...] = r
    v_o[...] = v.astype(v_o.dtype)
    kk_o[...] = kk
    g_o[...] = g
    bc = jnp.zeros_like(r)
    for d in range(2):
        wl = jnp.tanh(wl_ref[:, d * LORA:(d + 1) * LORA]).astype(BF16)
        w_raw = w0[d:d + 1, :] + jnp.dot(wl, w2[d], preferred_element_type=F32)
        lw_o[d] = jax.nn.sigmoid(w_raw) * (-float(np.exp(-0.5)))
        al = al_ref[:, d * LORA:(d + 1) * LORA].astype(BF16)
        a = jax.nn.sigmoid(a0[d:d + 1, :] + jnp.dot(al, a2[d], preferred_element_type=F32))
        kd = k * (1.0 + (a - 1.0) * kaw[...])
        kd_o[d] = kd
        b_o[d] = a * kk
        bc = bc + r * kd * rkw[...]
    bonus_o[...] = segsum(bc) * v


def _head_block_ones(width):
    i = np.arange(width) // RK
    return jnp.asarray((i[:, None] == i[None, :]).astype(np.float32)).astype(BF16)


def _head_sum(z, ones_bd):
    hi = z.astype(BF16)
    lo = (z - hi.astype(F32)).astype(BF16)
    return (jnp.dot(hi, ones_bd, preferred_element_type=F32)
            + jnp.dot(lo, ones_bd, preferred_element_type=F32))


def rwkv_prep(proj, p, tm=256, tc=512):
    n = proj.shape[0]
    nrb = n // 8
    tb = tm // 8

    def main(c0):
        return pl.BlockSpec((tm, tc), lambda i, j: (i, c0 // tc + j))

    def prev(c0):
        return pl.BlockSpec((8, tc), lambda i, j: (jnp.maximum(i * tb - 1, 0), c0 // tc + j))

    def nxt(c0):
        return pl.BlockSpec((8, tc), lambda i, j: (jnp.minimum((i + 1) * tb, nrb - 1), c0 // tc + j))

    def chan(rows, c0=0):
        return pl.BlockSpec((rows, tc), lambda i, j: (0, c0 // tc + j))

    in_specs = []
    for c0 in (C_R, C_RK, C_RV):
        in_specs += [main(c0 - C_R), prev(c0 - C_R), nxt(c0 - C_R)]
    in_specs += [
        pl.BlockSpec((tm, 2 * LORA), lambda i, j: (i, (C_WL - C_R) // (2 * LORA))),
        pl.BlockSpec((tm, 2 * LORA), lambda i, j: (i, (C_AL - C_R) // (2 * LORA))),
        pl.BlockSpec((tm, GATE_LORA_PAD), lambda i, j: (i, (C_GL - C_R) // GATE_LORA_PAD)),
        chan(3, 0), chan(3, RW), chan(3, 2 * RW),
        chan(2),
        pl.BlockSpec((2, LORA, tc), lambda i, j: (0, 0, j)),
        chan(2),
        pl.BlockSpec((2, LORA, tc), lambda i, j: (0, 0, j)),
        pl.BlockSpec((GATE_LORA_PAD, tc), lambda i, j: (0, j)),
        chan(1), chan(1), chan(1),
        pl.BlockSpec((tc, tc), lambda i, j: (0, 0)),
    ]
    one = pl.BlockSpec((tm, tc), lambda i, j: (i, j))
    two = pl.BlockSpec((2, tm, tc), lambda i, j: (0, i, j))
    s1 = jax.ShapeDtypeStruct((n, RW), F32)
    s2 = jax.ShapeDtypeStruct((2, n, RW), F32)
    return pl.pallas_call(
        functools.partial(_rwkv_prep_kernel, tm=tm),
        grid=(n // tm, RW // tc),
        in_specs=in_specs,
        out_specs=[one, one, one, one, one, two, two, two],
        out_shape=[s1, jax.ShapeDtypeStruct((n, RW), BF16), s1, s1, s1, s2, s2, s2],
        compiler_params=_cparams(("parallel", "parallel")),
        name="rwkv_prep",
    )(proj, proj, proj, proj, proj, proj, proj, proj, proj, proj, proj, proj,
      p["rwkv_conv"], p["rwkv_conv"], p["rwkv_conv"], p["decay_w0"], p["decay_w2"], p["iclr_a0"],
      p["iclr_a2"], p["gate_g2"], p["k_k"], p["k_a"], p["r_k"], _head_block_ones(tc))


def _scan_kernel(r_ref, v_ref, kk_ref, lw_ref, kd_ref, b_ref, y_ref, s_ref, *, pg, nc):
    d = pl.program_id(0)
    c = pl.program_id(2)
    chunk = c + d * (nc - 1 - 2 * c)
    npc = N_PROMPT // CHUNK
    in_p = chunk < npc
    per = jnp.where(in_p, P_SEQ // CHUNK, S_SEQ // CHUNK)
    rel = jnp.where(in_p, chunk, chunk - npc)
    @pl.when((rel + d) % per == 0)
    def _():
        s_ref[...] = jnp.zeros_like(s_ref)

    sgn = 1 - 2 * d
    C = CHUNK
    t_i = lax.broadcasted_iota(jnp.int32, (C, 2 * C), 0)
    s_i = lax.broadcasted_iota(jnp.int32, (C, 2 * C), 1) & (C - 1)
    rel_ts = (t_i - s_i) * sgn
    strict = rel_ts > 0
    incl = rel_ts >= 0
    lane = lax.broadcasted_iota(jnp.int32, (C, 2 * C), 1)
    m_a = lane < C
    tt = lax.broadcasted_iota(jnp.int32, (C, C), 0)
    ss = lax.broadcasted_iota(jnp.int32, (C, C), 1)
    tri = jnp.where((tt - ss) * sgn >= 0, 1.0, 0.0).astype(BF16)
    ri = lax.broadcasted_iota(jnp.int32, (2 * C, 2 * C), 0)
    ci = lax.broadcasted_iota(jnp.int32, (2 * C, 2 * C), 1)
    same_head = (ri >> 6) == (ci >> 6)
    eye = jnp.where(ri == ci, 1.0, 0.0).astype(F32)
    same8 = (ri >> 3) == (ci >> 3)
    lvl = [((ri >> (s + 1)) == (ci >> (s + 1))) & ((ri >> s) != (ci >> s)) for s in (3, 4, 5)]

    def split(x):
        return jnp.concatenate([jnp.where(m_a, x, 0.0), jnp.where(m_a, 0.0, x)], axis=0)

    def bf(x):
        return x.astype(BF16)

    def mm(a, b):
        return jnp.dot(a, b, preferred_element_type=F32)

    def mm_nt(a, b):
        return lax.dot_general(a, b, (((1,), (1,)), ((), ())), preferred_element_type=F32)

    def mm_tn(a, b):
        return lax.dot_general(a, b, (((0,), (0,)), ((), ())), preferred_element_type=F32)

    def cumsum(lw):
        hi = bf(lw)
        r1 = lw - hi.astype(F32)
        mid = bf(r1)
        lo = bf(r1 - mid.astype(F32))
        return mm(tri, hi) + (mm(tri, mid) + mm(tri, lo))

    P = range(pg)
    sls = [slice(p * 2 * C, (p + 1) * 2 * C) for p in P]
    r = [r_ref[:, sl] for sl in sls]
    v = [v_ref[:, sl] for sl in sls]
    kk = [kk_ref[:, sl] for sl in sls]
    lw = [lw_ref[0, :, sl] for sl in sls]
    kd = [kd_ref[0, :, sl] for sl in sls]
    b = [b_ref[0, :, sl] for sl in sls]
    cl = [cumsum(lw[p]) for p in P]
    tot = [jnp.sum(lw[p], axis=0, keepdims=True) for p in P]
    e_neg = [jnp.exp(-cl[p]) for p in P]
    e_end = [jnp.exp(tot[p] - cl[p]) for p in P]
    x = [bf(jnp.concatenate([r[p] * jnp.exp(cl[p]), kk[p] * jnp.exp(cl[p] - lw[p])], axis=0)) for p in P]
    z = [bf(jnp.concatenate([split(kd[p] * e_neg[p]), split(b[p] * e_neg[p])], axis=0)) for p in P]
    pm = [mm_nt(x[p], z[p]) for p in P]
    ld = [split(jnp.where(strict, pm[p][C:2 * C, 2 * C:4 * C], 0.0)) for p in P]
    l8 = [jnp.where(same8, ld[p], 0.0) for p in P]
    l8b = [bf(l8[p]) for p in P]
    l8_2 = [mm(l8b[p], l8b[p]) for p in P]
    l8_2b = [bf(l8_2[p]) for p in P]
    l8_4 = [mm(l8_2b[p], l8_2b[p]) for p in P]
    t1 = [mm(bf(eye - l8[p]), bf(eye + l8_2[p])) for p in P]
    inv = [mm(bf(t1[p]), bf(eye + l8_4[p])) for p in P]
    for msk in lvl:
        invb = [bf(inv[p]) for p in P]
        t2 = [mm(invb[p], bf(jnp.where(msk, ld[p], 0.0))) for p in P]
        inv = [inv[p] - mm(bf(t2[p]), invb[p]) for p in P]
    invb = [bf(inv[p]) for p in P]
    st = [s_ref[p] for p in P]
    xs = [mm_nt(x[p], bf(st[p])) for p in P]
    vs = [bf(split(v[p])) for p in P]
    rhs = [xs[p][C:2 * C] + mm(bf(jnp.where(strict, pm[p][C:2 * C, 0:2 * C], 0.0)), vs[p]) for p in P]
    ud = [mm(invb[p], bf(split(rhs[p]))) for p in P]
    for p in P:
        coef = jnp.concatenate([jnp.where(incl, pm[p][0:C, 0:2 * C], 0.0),
                                jnp.where(incl, -pm[p][0:C, 2 * C:4 * C], 0.0)], axis=1)
        y_ref[0, :, sls[p]] = xs[p][0:C] + mm(bf(coef), jnp.concatenate([vs[p], bf(ud[p])], axis=0))
    for p in P:
        u = ud[p][0:C] + ud[p][C:2 * C]
        upd = mm_tn(jnp.concatenate([v[p], bf(u)], axis=0),
                    bf(jnp.concatenate([kd[p] * e_end[p], -(b[p] * e_end[p])], axis=0)))
        s_ref[p] = st[p] * jnp.exp(tot[p]) + jnp.where(same_head, upd, 0.0)


def rwkv_scan(r, v, kk, lw, kd, b, pg=16):
    n = r.shape[0]
    nc = n // CHUNK
    w = pg * 2 * CHUNK

    def cmap(d, g, c):
        return (c + d * (nc - 1 - 2 * c), g)

    def dmap(d, g, c):
        return (d, c + d * (nc - 1 - 2 * c), g)

    one = pl.BlockSpec((CHUNK, w), cmap)
    two = pl.BlockSpec((1, CHUNK, w), dmap)
    return pl.pallas_call(
        functools.partial(_scan_kernel, pg=pg, nc=nc),
        grid=(2, RW // w, nc),
        in_specs=[one, one, one, two, two, two],
        out_specs=two,
        out_shape=jax.ShapeDtypeStruct((2, n, RW), F32),
        scratch_shapes=[pltpu.VMEM((pg, 2 * CHUNK, 2 * CHUNK), F32)],
        compiler_params=_cparams(("parallel", "parallel", "arbitrary")),
        name="rwkv_scan",
    )(r, v, kk, lw, kd, b)


def _rwkv_post_kernel(y_ref, bonus_ref, g_ref, w_ref, b_ref, bd_ref, o_ref):
    bd = bd_ref[...]

    def segmean(z):
        return _head_sum(z, bd) * (1.0 / RK)

    y = y_ref[0] + y_ref[1]
    yc = y - segmean(y)
    var = segmean(yc * yc)
    out = yc * lax.rsqrt(var + LNX_EPS) * w_ref[...] + b_ref[...]
    o_ref[...] = ((out + bonus_ref[...]) * g_ref[...]).astype(o_ref.dtype)


def rwkv_post(y, bonus, g, lnx_w, lnx_b, tm=256, tc=512):
    n = bonus.shape[0]
    one = pl.BlockSpec((tm, tc), lambda i, j: (i, j))
    ch = pl.BlockSpec((1, tc), lambda i, j: (0, j))
    return pl.pallas_call(
        _rwkv_post_kernel,
        grid=(n // tm, RW // tc),
        in_specs=[pl.BlockSpec((2, tm, tc), lambda i, j: (0, i, j)), one, one, ch, ch,
                  pl.BlockSpec((tc, tc), lambda i, j: (0, 0))],
        out_specs=one,
        out_shape=jax.ShapeDtypeStruct((n, RW), BF16),
        compiler_params=_cparams(("parallel", "parallel")),
        name="rwkv_post",
    )(y, bonus, g, lnx_w, lnx_b, _head_block_ones(tc))


HALO = 16


FF_TN = 256
FF_TILES = D_FF // FF_TN


def _ffn_up_kernel(h_ref, hp_ref, hn_ref, wv_ref, wg_ref, cv, cg, bv, bg, o_ref, wvb_ref, wgb_ref, *, tm):
    j = pl.program_id(0)

    @pl.when(j < FF_TILES)
    def _():
        _cast_weight_once(wv_ref, wvb_ref)
        _cast_weight_once(wg_ref, wgb_ref)
        row0 = pl.program_id(1) * tm
        h = h_ref[...]
        edge = jnp.concatenate([hp_ref[...], hn_ref[...]], axis=0)

        def branch(wb_ref, c_ref, b_ref):
            w = wb_ref[...]
            u = jnp.dot(h, w, preferred_element_type=F32)
            ue = jnp.dot(edge, w, preferred_element_type=F32)
            up, un = _neighbours(u, ue[HALO - 1:HALO, :], ue[HALO:HALO + 1, :], row0)
            return up * c_ref[0:1, :] + u * c_ref[1:2, :] + un * c_ref[2:3, :] + b_ref[...]

        gate = branch(wgb_ref, cg, bg)
        act = gate * jax.nn.sigmoid(gate)
        o_ref[...] = (act * branch(wvb_ref, cv, bv)).astype(o_ref.dtype)

    @pl.when(j == FF_TILES)
    def _():
        o_ref[...] = jnp.zeros_like(o_ref)


def ffn_up_act(h, w_up, conv_w, conv_b, layer, tm=1024):
    n, k = h.shape
    tn = FF_TN
    nhb = n // HALO
    tb = tm // HALO

    def col(j):
        return jnp.minimum(j, FF_TILES - 1)

    def chan(rows, o):
        return pl.BlockSpec((None, rows, tn), lambda j, i: (layer, 0, o + col(j)))

    return pl.pallas_call(
        functools.partial(_ffn_up_kernel, tm=tm),
        grid=(D_FF_PAD // tn, n // tm),
        in_specs=[
            pl.BlockSpec((tm, k), lambda j, i: (i, 0)),
            pl.BlockSpec((HALO, k), lambda j, i: (jnp.maximum(i * tb - 1, 0), 0)),
            pl.BlockSpec((HALO, k), lambda j, i: (jnp.minimum((i + 1) * tb, nhb - 1), 0)),
            chan(k, 0), chan(k, FF_TILES),
            chan(3, 0), chan(3, FF_TILES), chan(1, 0), chan(1, FF_TILES),
        ],
        out_specs=pl.BlockSpec((tm, tn), lambda j, i: (i, j)),
        out_shape=jax.ShapeDtypeStruct((n, D_FF_PAD), BF16),
        scratch_shapes=[pltpu.VMEM((k, tn), BF16), pltpu.VMEM((k, tn), BF16)],
        compiler_params=_cparams(("parallel", "arbitrary")),
        name="ffn_up_act",
    )(h, h, h, w_up, w_up, conv_w, conv_w, conv_b, conv_b)


def _layer_params(l, norm_mix, q_gain, k_gain, rwkv_conv, decay_w0, decay_w2, iclr_a0, iclr_a2,
                  gate_g2, k_k, k_a, r_k, lnx_w, lnx_b, norm_ffn, w_ffn_down):
    gains = jnp.stack([q_gain[l] * (HEAD ** -0.5 * float(np.log2(np.e)))] * 4 + [k_gain[l]] * 2)[:, None, :]
    return dict(
        norm_mix=norm_mix[l], gains=gains, rwkv_conv=rwkv_conv[l],
        decay_w0=decay_w0[l], decay_w2=decay_w2[l].astype(BF16),
        iclr_a0=iclr_a0[l], iclr_a2=iclr_a2[l].astype(BF16),
        gate_g2=jnp.pad(gate_g2[l], ((0, GATE_LORA_PAD - GATE_LORA), (0, 0))).astype(BF16),
        k_k=k_k[l][None, :], k_a=k_a[l][None, :], r_k=r_k[l].reshape(1, RW),
        lnx_w=lnx_w[l][None, :], lnx_b=lnx_b[l][None, :], norm_ffn=norm_ffn[l],
        w_ffn_down=jnp.pad(w_ffn_down[l], ((0, D_FF_PAD - D_FF), (0, 0))).astype(BF16),
    )


def _layer(x, l, p, big, tables, last):
    h = rmsnorm_cast(x, p["norm_mix"])
    qkv = qkv_proj(h, big["w_in"], l, p["gains"], tables)
    proj = matmul(h, big["w_in"], l, C_GA - C_R, col0=C_R, name="in_proj")
    gates = matmul(h, big["w_gates"], l, 2 * D_MODEL, name="gate_proj")
    att = jnp.concatenate([attention(qkv, 0, P_BATCH, P_SEQ),
                           attention(qkv, N_PROMPT, S_BATCH, S_SEQ)], axis=0)
    r, v, kk, bonus, g, lw, kd, b = rwkv_prep(proj, p)
    y = rwkv_scan(r, v, kk, lw, kd, b)
    rw = rwkv_post(y, bonus, g, p["lnx_w"], p["lnx_b"])
    mixed = merge_branches(att, rw, big["w_up_attn"], big["w_up_rwkv"], l, gates)
    x = matmul(mixed, big["w_o"], l, D_MODEL, res=x, name="out_proj")
    h = rmsnorm_cast(x, p["norm_ffn"])
    act = ffn_up_act(h, big["w_ffn_up"], big["ffn_conv"], big["ffn_conv_b"], l)
    if not last:
        return matmul_ktiled_res(act, p["w_ffn_down"], x, name="ffn_down")
    return (matmul_ktiled_res(act, p["w_ffn_down"], x, row0=0, rows=N_PROMPT, name="ffn_down_prompt"),
            matmul_ktiled_res(act, p["w_ffn_down"], x, row0=N_PROMPT, rows=N_TOK - N_PROMPT, name="ffn_down_sample"))


def kernel(x_prompt, x_sample, norm_mix, w_in, q_gain, k_gain, rwkv_conv, decay_w0, decay_w2, iclr_a0, iclr_a2, gate_g2, k_k, k_a, r_k, lnx_w, lnx_b, w_up_attn, w_up_rwkv, w_o, norm_ffn, w_ffn_up, ffn_conv, ffn_conv_b, w_ffn_down):
    small = (norm_mix, q_gain, k_gain, rwkv_conv, decay_w0, decay_w2, iclr_a0, iclr_a2,
             gate_g2, k_k, k_a, r_k, lnx_w, lnx_b, norm_ffn, w_ffn_down)
    big = dict(w_in=w_in[:, :, :C_GA], w_gates=w_in[:, :, IN_GATES:],
               w_up_attn=w_up_attn, w_up_rwkv=w_up_rwkv, w_o=w_o,
               w_ffn_up=w_ffn_up, ffn_conv=ffn_conv, ffn_conv_b=ffn_conv_b[:, None, :])
    x = jnp.concatenate([x_prompt.reshape(N_PROMPT, D_MODEL), x_sample.reshape(-1, D_MODEL)], axis=0)
    tables = _rope_tables()
    depth = norm_mix.shape[0]
    for l in range(depth):
        x = _layer(x, l, _layer_params(l, *small), big, tables, last=l == depth - 1)
    y_prompt, y_sample = x
    return (y_prompt.reshape(P_BATCH, P_SEQ, D_MODEL), y_sample.reshape(S_BATCH, S_SEQ, D_MODEL))
```

```python
import functools

import jax
import jax.numpy as jnp
import numpy as np
from jax import lax
from jax.experimental import pallas as pl
from jax.experimental.pallas import tpu as pltpu

F32 = jnp.float32
BF16 = jnp.bfloat16

D_MODEL = 4096
P_BATCH, P_SEQ = 4, 2048
S_BATCH, S_SEQ = 2, 4096
N_PROMPT = P_BATCH * P_SEQ
N_TOK = N_PROMPT + S_BATCH * S_SEQ
GRID_W = 64
HEAD = 128
ATT_W = 2048
KV_W = 512
KV_HEADS = 4
GROUPS = 4
ROPE_THETA = 10000.0
RK = 64
RW = 2048
LORA = 128
GATE_LORA = 480
GATE_LORA_PAD = 512
D_FF = 11008
D_FF_PAD = 11264
NORM_EPS = 1e-6
LNX_EPS = RK * 1e-5

C_Q, C_K, C_V = 0, 2048, 2560
C_R, C_RK, C_RV = 3072, 5120, 7168
C_WL, C_AL, C_GL = 9216, 9472, 9728
C_GA = C_GL + GATE_LORA_PAD
IN_GATES = C_GL + GATE_LORA

CHUNK = 64
VMEM_LIMIT = 56 * 1024 * 1024


def _cparams(sem):
    return pltpu.CompilerParams(dimension_semantics=sem, vmem_limit_bytes=VMEM_LIMIT)


def _seq_edges(rows):
    pos = rows & jnp.where(rows < N_PROMPT, P_SEQ - 1, S_SEQ - 1)
    last = jnp.where(rows < N_PROMPT, P_SEQ - 1, S_SEQ - 1)
    return pos == 0, pos == last


def _neighbours(x, prev_row, next_row, row0):
    tm = x.shape[0]
    assert P_SEQ % tm == 0 and S_SEQ % tm == 0
    li = lax.broadcasted_iota(jnp.int32, x.shape, 0)
    starts, _ = _seq_edges(row0)
    _, ends = _seq_edges(row0 + (tm - 1))
    prev_row = jnp.where(starts, 0.0, prev_row)
    next_row = jnp.where(ends, 0.0, next_row)
    xp = jnp.where(li == 0, prev_row, pltpu.roll(x, 1, axis=0))
    xn = jnp.where(li == tm - 1, next_row, pltpu.roll(x, tm - 1, axis=0))
    return xp, xn


def _rmsnorm_kernel(x_ref, g_ref, o_ref):
    x = x_ref[...]
    ms = jnp.mean(x * x, axis=-1, keepdims=True)
    o_ref[...] = (x * lax.rsqrt(ms + NORM_EPS) * g_ref[...]).astype(o_ref.dtype)


def rmsnorm_cast(x, g, tm=256):
    n, d = x.shape
    return pl.pallas_call(
        _rmsnorm_kernel,
        grid=(n // tm,),
        in_specs=[pl.BlockSpec((tm, d), lambda i: (i, 0)), pl.BlockSpec((1, d), lambda i: (0, 0))],
        out_specs=pl.BlockSpec((tm, d), lambda i: (i, 0)),
        out_shape=jax.ShapeDtypeStruct((n, d), BF16),
        compiler_params=_cparams(("parallel",)),
        name="rmsnorm_cast",
    )(x, g.reshape(1, d))


def _cast_weight_once(w_ref, wb_ref):
    @pl.when(pl.program_id(1) == 0)
    def _():
        wb_ref[...] = w_ref[...].astype(BF16)


def _mm_kernel(a_ref, b_ref, o_ref, bb_ref):
    _cast_weight_once(b_ref, bb_ref)
    o_ref[...] = jnp.dot(a_ref[...], bb_ref[...], preferred_element_type=F32).astype(o_ref.dtype)


def _mm_res_kernel(a_ref, b_ref, r_ref, o_ref, bb_ref):
    _cast_weight_once(b_ref, bb_ref)
    o_ref[...] = r_ref[...] + jnp.dot(a_ref[...], bb_ref[...], preferred_element_type=F32)


def matmul(a, w, layer, n_cols, res=None, *, col0=0, tm=512, tn=512, name="matmul"):
    m, k = a.shape
    cb0 = col0 // tn
    in_specs = [pl.BlockSpec((tm, k), lambda j, i: (i, 0)),
                pl.BlockSpec((None, k, tn), lambda j, i: (layer, 0, cb0 + j))]
    args = [a, w]
    kern = _mm_kernel
    if res is not None:
        in_specs.append(pl.BlockSpec((tm, tn), lambda j, i: (i, j)))
        args.append(res)
        kern = _mm_res_kernel
    return pl.pallas_call(
        kern,
        grid=(n_cols // tn, m // tm),
        in_specs=in_specs,
        out_specs=pl.BlockSpec((tm, tn), lambda j, i: (i, j)),
        out_shape=jax.ShapeDtypeStruct((m, n_cols), F32),
        scratch_shapes=[pltpu.VMEM((k, tn), BF16)],
        compiler_params=_cparams(("parallel", "arbitrary")),
        name=name,
    )(*args)


def _dot_nt(a, wt):
    return lax.dot_general(a, wt, (((1,), (1,)), ((), ())), preferred_element_type=F32)


def _mm_nt_kernel(a_ref, w_ref, o_ref, wb_ref):
    _cast_weight_once(w_ref, wb_ref)
    o_ref[...] = _dot_nt(a_ref[...], wb_ref[...])


def matmul_nt(a, wt, layer, n_cols, *, row0=0, tm=512, tn=512, name="matmul_nt"):
    m, k = a.shape
    rb0 = row0 // tn
    return pl.pallas_call(
        _mm_nt_kernel,
        grid=(n_cols // tn, m // tm),
        in_specs=[pl.BlockSpec((tm, k), lambda j, i: (i, 0)),
                  pl.BlockSpec((None, tn, k), lambda j, i: (layer, rb0 + j, 0))],
        out_specs=pl.BlockSpec((tm, tn), lambda j, i: (i, j)),
        out_shape=jax.ShapeDtypeStruct((m, n_cols), F32),
        scratch_shapes=[pltpu.VMEM((tn, k), BF16)],
        compiler_params=_cparams(("parallel", "arbitrary")),
        name=name,
    )(a, wt)


def _gate_proj_kernel(a_ref, wl_ref, wr_ref, o_ref, wb_ref, *, shift):
    @pl.when(pl.program_id(1) == 0)
    def _():
        tn = wb_ref.shape[0]
        wb_ref[0:tn - shift, :] = wl_ref[shift:tn, :].astype(BF16)
        wb_ref[tn - shift:tn, :] = wr_ref[0:shift, :].astype(BF16)

    o_ref[...] = _dot_nt(a_ref[...], wb_ref[...])


def gate_proj(a, wt, layer, tm=1024, tn=256):
    m, k = a.shape
    rb0, shift = IN_GATES // tn, IN_GATES % tn
    assert shift % 16 == 0 and (tn - shift) % 16 == 0
    return pl.pallas_call(
        functools.partial(_gate_proj_kernel, shift=shift),
        grid=(2 * D_MODEL // tn, m // tm),
        in_specs=[pl.BlockSpec((tm, k), lambda j, i: (i, 0)),
                  pl.BlockSpec((None, tn, k), lambda j, i: (layer, rb0 + j, 0)),
                  pl.BlockSpec((None, tn, k), lambda j, i: (layer, rb0 + j + 1, 0))],
        out_specs=pl.BlockSpec((tm, tn), lambda j, i: (i, j)),
        out_shape=jax.ShapeDtypeStruct((m, 2 * D_MODEL), F32),
        scratch_shapes=[pltpu.VMEM((tn, k), BF16)],
        compiler_params=_cparams(("parallel", "arbitrary")),
        name="gate_proj",
    )(a, wt, wt)


def _mmk_res_kernel(a_ref, b_ref, r_ref, o_ref, acc_ref, *, k_rows):
    kk = pl.program_id(2)
    tk = b_ref.shape[0]

    @pl.when(kk == 0)
    def _():
        acc_ref[...] = r_ref[...]

    row = lax.broadcasted_iota(jnp.int32, b_ref.shape, 0)
    b = jnp.where(row < k_rows - kk * tk, b_ref[...], 0.0).astype(BF16)
    acc_ref[...] += jnp.dot(a_ref[...], b, preferred_element_type=F32)

    @pl.when(kk == pl.num_programs(2) - 1)
    def _():
        o_ref[...] = acc_ref[...]


def matmul_ktiled_res(a, w, layer, res, *, row0=0, rows=None, tm=1024, tn=1024, tk=1408, name="matmul_k"):
    k = a.shape[1]
    m = a.shape[0] if rows is None else rows
    n = w.shape[2]
    rb0 = row0 // tm
    return pl.pallas_call(
        functools.partial(_mmk_res_kernel, k_rows=w.shape[1]),
        grid=(m // tm, n // tn, k // tk),
        in_specs=[
            pl.BlockSpec((tm, tk), lambda i, j, kk: (rb0 + i, kk)),
            pl.BlockSpec((None, tk, tn), lambda i, j, kk: (layer, kk, j)),
            pl.BlockSpec((tm, tn), lambda i, j, kk: (rb0 + i, j)),
        ],
        out_specs=pl.BlockSpec((tm, tn), lambda i, j, kk: (i, j)),
        out_shape=jax.ShapeDtypeStruct((m, n), F32),
        scratch_shapes=[pltpu.VMEM((tm, tn), F32)],
        compiler_params=_cparams(("parallel", "parallel", "arbitrary")),
        name=name,
    )(a, w, res)


def _merge_kernel(att_ref, rw_ref, wa_ref, wr_ref, ga_ref, gr_ref, o_ref, wab_ref, wrb_ref):
    _cast_weight_once(wa_ref, wab_ref)
    _cast_weight_once(wr_ref, wrb_ref)
    ya = jnp.dot(att_ref[...], wab_ref[...], preferred_element_type=F32)
    yr = jnp.dot(rw_ref[...], wrb_ref[...], preferred_element_type=F32)
    o_ref[...] = (jax.nn.sigmoid(ga_ref[...]) * ya + jax.nn.sigmoid(gr_ref[...]) * yr).astype(o_ref.dtype)


def merge_branches(att, rw, wa, wr, layer, gates, tm=512, tn=512):
    n = att.shape[0]
    gr0 = D_MODEL // tn
    return pl.pallas_call(
        _merge_kernel,
        grid=(D_MODEL // tn, n // tm),
        in_specs=[
            pl.BlockSpec((tm, ATT_W), lambda j, i: (i, 0)),
            pl.BlockSpec((tm, RW), lambda j, i: (i, 0)),
            pl.BlockSpec((None, ATT_W, tn), lambda j, i: (layer, 0, j)),
            pl.BlockSpec((None, RW, tn), lambda j, i: (layer, 0, j)),
            pl.BlockSpec((tm, tn), lambda j, i: (i, j)),
            pl.BlockSpec((tm, tn), lambda j, i: (i, gr0 + j)),
        ],
        out_specs=pl.BlockSpec((tm, tn), lambda j, i: (i, j)),
        out_shape=jax.ShapeDtypeStruct((n, D_MODEL), BF16),
        scratch_shapes=[pltpu.VMEM((ATT_W, tn), BF16), pltpu.VMEM((RW, tn), BF16)],
        compiler_params=_cparams(("parallel", "arbitrary")),
        name="merge_branches",
    )(att, rw, wa, wr, gates, gates)


def _qkv_proj_kernel(a_ref, w_ref, gain_ref, cos_ref, sa_ref, sb_ref, o_ref, wb_ref):
    _cast_weight_once(w_ref, wb_ref)
    acc = _dot_nt(a_ref[...], wb_ref[...])
    j = pl.program_id(0)

    @pl.when(j < 5)
    def _():
        cos, sa, sb = cos_ref[...], sa_ref[...], sb_ref[...]
        gain = gain_ref[0]
        for h in range(4):
            x = acc[:, h * HEAD:(h + 1) * HEAD]
            ms = jnp.mean(x * x, axis=-1, keepdims=True)
            y = x * lax.rsqrt(ms + NORM_EPS) * gain
            y = y * cos + pltpu.roll(y, 96, axis=1) * sa + pltpu.roll(y, 32, axis=1) * sb
            o_ref[:, h * HEAD:(h + 1) * HEAD] = y.astype(o_ref.dtype)

    @pl.when(j == 5)
    def _():
        o_ref[...] = acc.astype(o_ref.dtype)


def _rope_tables():
    t = np.arange(S_SEQ)
    row = (t // GRID_W).astype(np.float32)
    col = (t % GRID_W).astype(np.float32)
    axis_dim = HEAD // 2
    inv = jnp.asarray(ROPE_THETA, F32) ** (-jnp.arange(0, axis_dim, 2, dtype=F32) / axis_dim)
    ang_r = jnp.asarray(row)[:, None] * inv
    ang_c = jnp.asarray(col)[:, None] * inv
    cr, sr, cc, sc = jnp.cos(ang_r), jnp.sin(ang_r), jnp.cos(ang_c), jnp.sin(ang_c)
    z = jnp.zeros_like(sr)
    cos = jnp.concatenate([cr, cr, cc, cc], axis=1)
    sin_a = jnp.concatenate([-sr, z, -sc, z], axis=1)
    sin_b = jnp.concatenate([z, sr, z, sc], axis=1)
    return cos, sin_a, sin_b


def qkv_proj(h, w_in, layer, gains, tables, tm=512, tn=512):
    n, k = h.shape
    npb = N_PROMPT // tm

    def tab_map(j, i):
        return (jnp.where(i < npb, i % (P_SEQ // tm), (i - npb) % (S_SEQ // tm)), 0)

    tab_spec = pl.BlockSpec((tm, HEAD), tab_map)
    return pl.pallas_call(
        _qkv_proj_kernel,
        grid=(C_R // tn, n // tm),
        in_specs=[
            pl.BlockSpec((tm, k), lambda j, i: (i, 0)),
            pl.BlockSpec((None, tn, k), lambda j, i: (layer, j, 0)),
            pl.BlockSpec((1, 1, HEAD), lambda j, i: (j, 0, 0)),
            tab_spec, tab_spec, tab_spec,
        ],
        out_specs=pl.BlockSpec((tm, tn), lambda j, i: (i, j)),
        out_shape=jax.ShapeDtypeStruct((n, C_R), BF16),
        scratch_shapes=[pltpu.VMEM((tn, k), BF16)],
        compiler_params=_cparams(("parallel", "arbitrary")),
        name="qkv_proj",
    )(h, w_in, gains, *tables)


def _attn_kernel(q_ref, k_ref, v_ref, o_ref, v1_ref):
    @pl.when(pl.program_id(2) == 0)
    def _():
        v1_ref[:, 0:HEAD] = v_ref[...]
        v1_ref[:, HEAD:2 * HEAD] = jnp.ones(v_ref.shape, v1_ref.dtype)

    k = k_ref[...]
    v1 = v1_ref[...]
    for g in range(GROUPS):
        q = q_ref[:, g * HEAD:(g + 1) * HEAD]
        s = lax.dot_general(q, k, (((1,), (1,)), ((), ())), preferred_element_type=F32)
        p = jnp.exp2(s - jnp.max(s, axis=-1, keepdims=True))
        ol = jnp.dot(p.astype(BF16), v1, preferred_element_type=F32)
        o_ref[:, g * HEAD:(g + 1) * HEAD] = (ol[:, 0:HEAD] / ol[:, HEAD:2 * HEAD]).astype(o_ref.dtype)


def attention(qkv, row0, batch, seq, tq=256):
    qb0 = row0 // tq
    sb0 = row0 // seq
    nq = seq // tq
    return pl.pallas_call(
        _attn_kernel,
        grid=(batch, KV_HEADS, nq),
        in_specs=[
            pl.BlockSpec((tq, GROUPS * HEAD), lambda b, h, i: (qb0 + b * nq + i, h)),
            pl.BlockSpec((seq, HEAD), lambda b, h, i: (sb0 + b, C_K // HEAD + h)),
            pl.BlockSpec((seq, HEAD), lambda b, h, i: (sb0 + b, C_V // HEAD + h)),
        ],
        out_specs=pl.BlockSpec((tq, GROUPS * HEAD), lambda b, h, i: (b * nq + i, h)),
        out_shape=jax.ShapeDtypeStruct((batch * seq, ATT_W), BF16),
        scratch_shapes=[pltpu.VMEM((seq, 2 * HEAD), BF16)],
        compiler_params=_cparams(("parallel", "parallel", "arbitrary")),
        name="attention",
    )(qkv, qkv, qkv)


def _rwkv_prep_kernel(xr, xr_p, xr_n, xk, xk_p, xk_n, xv, xv_p, xv_n, wl_ref, al_ref, gl_ref,
                      cr, ck, cv, w0, w2, a0, a2, g2, kkw, kaw, rkw, bd_ref,
                      r_o, v_o, kk_o, bonus_o, g_o, lw_o, kd_o, b_o, *, tm):
    row0 = pl.program_id(0) * tm

    def conv(x_ref, p_ref, n_ref, c_ref):
        x = x_ref[...]
        xp, xn = _neighbours(x, p_ref[7:8, :], n_ref[0:1, :], row0)
        return xp * c_ref[0:1, :] + x * c_ref[1:2, :] + xn * c_ref[2:3, :]

    r = conv(xr, xr_p, xr_n, cr)
    k = conv(xk, xk_p, xk_n, ck)
    v = conv(xv, xv_p, xv_n, cv)
    bd = bd_ref[...]

    def segsum(z):
        return _head_sum(z, bd)

    kkr = k * kkw[...]
    kk = kkr / jnp.maximum(jnp.sqrt(segsum(kkr * kkr)), 1e-12)
    g = jnp.dot(jax.nn.sigmoid(gl_ref[...]).astype(BF16), g2[...], preferred_element_type=F32)
    r_o[...] = r
    v_o[...] = v.astype(v_o.dtype)
    kk_o[...] = kk
    g_o[...] = g
    bc = jnp.zeros_like(r)
    for d in range(2):
        wl = jnp.tanh(wl_ref[:, d * LORA:(d + 1) * LORA]).astype(BF16)
        w_raw = w0[d:d + 1, :] + jnp.dot(wl, w2[d], preferred_element_type=F32)
        lw_o[d] = jax.nn.sigmoid(w_raw) * (-float(np.exp(-0.5)))
        al = al_ref[:, d * LORA:(d + 1) * LORA].astype(BF16)
        a = jax.nn.sigmoid(a0[d:d + 1, :] + jnp.dot(al, a2[d], preferred_element_type=F32))
        kd = k * (1.0 + (a - 1.0) * kaw[...])
        kd_o[d] = kd
        b_o[d] = a * kk
        bc = bc + r * kd * rkw[...]
    bonus_o[...] = segsum(bc) * v


def _head_block_ones(width):
    i = np.arange(width) // RK
    return jnp.asarray((i[:, None] == i[None, :]).astype(np.float32)).astype(BF16)


def _head_sum(z, ones_bd):
    hi = z.astype(BF16)
    lo = (z - hi.astype(F32)).astype(BF16)
    return (jnp.dot(hi, ones_bd, preferred_element_type=F32)
            + jnp.dot(lo, ones_bd, preferred_element_type=F32))


def rwkv_prep(proj, p, tm=256, tc=512):
    n = proj.shape[0]
    nrb = n // 8
    tb = tm // 8

    def main(c0):
        return pl.BlockSpec((tm, tc), lambda i, j: (i, c0 // tc + j))

    def prev(c0):
        return pl.BlockSpec((8, tc), lambda i, j: (jnp.maximum(i * tb - 1, 0), c0 // tc + j))

    def nxt(c0):
        return pl.BlockSpec((8, tc), lambda i, j: (jnp.minimum((i + 1) * tb, nrb - 1), c0 // tc + j))

    def chan(rows, c0=0):
        return pl.BlockSpec((rows, tc), lambda i, j: (0, c0 // tc + j))

    in_specs = []
    for c0 in (C_R, C_RK, C_RV):
        in_specs += [main(c0 - C_R), prev(c0 - C_R), nxt(c0 - C_R)]
    in_specs += [
        pl.BlockSpec((tm, 2 * LORA), lambda i, j: (i, (C_WL - C_R) // (2 * LORA))),
        pl.BlockSpec((tm, 2 * LORA), lambda i, j: (i, (C_AL - C_R) // (2 * LORA))),
        pl.BlockSpec((tm, GATE_LORA_PAD), lambda i, j: (i, (C_GL - C_R) // GATE_LORA_PAD)),
        chan(3, 0), chan(3, RW), chan(3, 2 * RW),
        chan(2),
        pl.BlockSpec((2, LORA, tc), lambda i, j: (0, 0, j)),
        chan(2),
        pl.BlockSpec((2, LORA, tc), lambda i, j: (0, 0, j)),
        pl.BlockSpec((GATE_LORA_PAD, tc), lambda i, j: (0, j)),
        chan(1), chan(1), chan(1),
        pl.BlockSpec((tc, tc), lambda i, j: (0, 0)),
    ]
    one = pl.BlockSpec((tm, tc), lambda i, j: (i, j))
    two = pl.BlockSpec((2, tm, tc), lambda i, j: (0, i, j))
    s1 = jax.ShapeDtypeStruct((n, RW), F32)
    s2 = jax.ShapeDtypeStruct((2, n, RW), F32)
    return pl.pallas_call(
        functools.partial(_rwkv_prep_kernel, tm=tm),
        grid=(n // tm, RW // tc),
        in_specs=in_specs,
        out_specs=[one, one, one, one, one, two, two, two],
        out_shape=[s1, jax.ShapeDtypeStruct((n, RW), BF16), s1, s1, s1, s2, s2, s2],
        compiler_params=_cparams(("parallel", "parallel")),
        name="rwkv_prep",
    )(proj, proj, proj, proj, proj, proj, proj, proj, proj, proj, proj, proj,
      p["rwkv_conv"], p["rwkv_conv"], p["rwkv_conv"], p["decay_w0"], p["decay_w2"], p["iclr_a0"],
      p["iclr_a2"], p["gate_g2"], p["k_k"], p["k_a"], p["r_k"], _head_block_ones(tc))


def _scan_kernel(r_ref, v_ref, kk_ref, lw_ref, kd_ref, b_ref, y_ref, s_ref, *, pg, nc):
    d = pl.program_id(0)
    c = pl.program_id(2)
    chunk = c + d * (nc - 1 - 2 * c)
    npc = N_PROMPT // CHUNK
    in_p = chunk < npc
    per = jnp.where(in_p, P_SEQ // CHUNK, S_SEQ // CHUNK)
    rel = jnp.where(in_p, chunk, chunk - npc)
    @pl.when((rel + d) % per == 0)
    def _():
        s_ref[...] = jnp.zeros_like(s_ref)

    sgn = 1 - 2 * d
    C = CHUNK
    t_i = lax.broadcasted_iota(jnp.int32, (C, 2 * C), 0)
    s_i = lax.broadcasted_iota(jnp.int32, (C, 2 * C), 1) & (C - 1)
    rel_ts = (t_i - s_i) * sgn
    strict = rel_ts > 0
    incl = rel_ts >= 0
    lane = lax.broadcasted_iota(jnp.int32, (C, 2 * C), 1)
    m_a = lane < C
    tt = lax.broadcasted_iota(jnp.int32, (C, C), 0)
    ss = lax.broadcasted_iota(jnp.int32, (C, C), 1)
    tri = jnp.where((tt - ss) * sgn >= 0, 1.0, 0.0).astype(BF16)
    ri = lax.broadcasted_iota(jnp.int32, (2 * C, 2 * C), 0)
    ci = lax.broadcasted_iota(jnp.int32, (2 * C, 2 * C), 1)
    same_head = (ri >> 6) == (ci >> 6)
    eye = jnp.where(ri == ci, 1.0, 0.0).astype(F32)
    same8 = (ri >> 3) == (ci >> 3)
    lvl = [((ri >> (s + 1)) == (ci >> (s + 1))) & ((ri >> s) != (ci >> s)) for s in (3, 4, 5)]

    def split(x):
        return jnp.concatenate([jnp.where(m_a, x, 0.0), jnp.where(m_a, 0.0, x)], axis=0)

    def bf(x):
        return x.astype(BF16)

    def mm(a, b):
        return jnp.dot(a, b, preferred_element_type=F32)

    def mm_nt(a, b):
        return lax.dot_general(a, b, (((1,), (1,)), ((), ())), preferred_element_type=F32)

    def mm_tn(a, b):
        return lax.dot_general(a, b, (((0,), (0,)), ((), ())), preferred_element_type=F32)

    def cumsum(lw):
        hi = bf(lw)
        r1 = lw - hi.astype(F32)
        mid = bf(r1)
        lo = bf(r1 - mid.astype(F32))
        return mm(tri, hi) + (mm(tri, mid) + mm(tri, lo))

    P = range(pg)
    sls = [slice(p * 2 * C, (p + 1) * 2 * C) for p in P]
    r = [r_ref[:, sl] for sl in sls]
    v = [v_ref[:, sl] for sl in sls]
    kk = [kk_ref[:, sl] for sl in sls]
    lw = [lw_ref[0, :, sl] for sl in sls]
    kd = [kd_ref[0, :, sl] for sl in sls]
    b = [b_ref[0, :, sl] for sl in sls]
    cl = [cumsum(lw[p]) for p in P]
    tot = [jnp.sum(lw[p], axis=0, keepdims=True) for p in P]
    e_neg = [jnp.exp(-cl[p]) for p in P]
    e_end = [jnp.exp(tot[p] - cl[p]) for p in P]
    x = [bf(jnp.concatenate([r[p] * jnp.exp(cl[p]), kk[p] * jnp.exp(cl[p] - lw[p])], axis=0)) for p in P]
    z = [bf(jnp.concatenate([split(kd[p] * e_neg[p]), split(b[p] * e_neg[p])], axis=0)) for p in P]
    pm = [mm_nt(x[p], z[p]) for p in P]
    ld = [split(jnp.where(strict, pm[p][C:2 * C, 2 * C:4 * C], 0.0)) for p in P]
    l8 = [jnp.where(same8, ld[p], 0.0) for p in P]
    l8b = [bf(l8[p]) for p in P]
    l8_2 = [mm(l8b[p], l8b[p]) for p in P]
    l8_2b = [bf(l8_2[p]) for p in P]
    l8_4 = [mm(l8_2b[p], l8_2b[p]) for p in P]
    t1 = [mm(bf(eye - l8[p]), bf(eye + l8_2[p])) for p in P]
    inv = [mm(bf(t1[p]), bf(eye + l8_4[p])) for p in P]
    for msk in lvl:
        invb = [bf(inv[p]) for p in P]
        t2 = [mm(invb[p], bf(jnp.where(msk, ld[p], 0.0))) for p in P]
        inv = [inv[p] - mm(bf(t2[p]), invb[p]) for p in P]
    invb = [bf(inv[p]) for p in P]
    st = [s_ref[p] for p in P]
    xs = [mm_nt(x[p], bf(st[p])) for p in P]
    vs = [bf(split(v[p])) for p in P]
    rhs = [xs[p][C:2 * C] + mm(bf(jnp.where(strict, pm[p][C:2 * C, 0:2 * C], 0.0)), vs[p]) for p in P]
    ud = [mm(invb[p], bf(split(rhs[p]))) for p in P]
    for p in P:
        coef = jnp.concatenate([jnp.where(incl, pm[p][0:C, 0:2 * C], 0.0),
                                jnp.where(incl, -pm[p][0:C, 2 * C:4 * C], 0.0)], axis=1)
        y_ref[0, :, sls[p]] = xs[p][0:C] + mm(bf(coef), jnp.concatenate([vs[p], bf(ud[p])], axis=0))
    for p in P:
        u = ud[p][0:C] + ud[p][C:2 * C]
        upd = mm_tn(jnp.concatenate([v[p], bf(u)], axis=0),
                    bf(jnp.concatenate([kd[p] * e_end[p], -(b[p] * e_end[p])], axis=0)))
        s_ref[p] = st[p] * jnp.exp(tot[p]) + jnp.where(same_head, upd, 0.0)


def rwkv_scan(r, v, kk, lw, kd, b, pg=16):
    n = r.shape[0]
    nc = n // CHUNK
    w = pg * 2 * CHUNK

    def cmap(d, g, c):
        return (c + d * (nc - 1 - 2 * c), g)

    def dmap(d, g, c):
        return (d, c + d * (nc - 1 - 2 * c), g)

    one = pl.BlockSpec((CHUNK, w), cmap)
    two = pl.BlockSpec((1, CHUNK, w), dmap)
    return pl.pallas_call(
        functools.partial(_scan_kernel, pg=pg, nc=nc),
        grid=(2, RW // w, nc),
        in_specs=[one, one, one, two, two, two],
        out_specs=two,
        out_shape=jax.ShapeDtypeStruct((2, n, RW), F32),
        scratch_shapes=[pltpu.VMEM((pg, 2 * CHUNK, 2 * CHUNK), F32)],
        compiler_params=_cparams(("parallel", "parallel", "arbitrary")),
        name="rwkv_scan",
    )(r, v, kk, lw, kd, b)


def _rwkv_post_kernel(y_ref, bonus_ref, g_ref, w_ref, b_ref, bd_ref, o_ref):
    bd = bd_ref[...]

    def segmean(z):
        return _head_sum(z, bd) * (1.0 / RK)

    y = y_ref[0] + y_ref[1]
    yc = y - segmean(y)
    var = segmean(yc * yc)
    out = yc * lax.rsqrt(var + LNX_EPS) * w_ref[...] + b_ref[...]
    o_ref[...] = ((out + bonus_ref[...]) * g_ref[...]).astype(o_ref.dtype)


def rwkv_post(y, bonus, g, lnx_w, lnx_b, tm=256, tc=512):
    n = bonus.shape[0]
    one = pl.BlockSpec((tm, tc), lambda i, j: (i, j))
    ch = pl.BlockSpec((1, tc), lambda i, j: (0, j))
    return pl.pallas_call(
        _rwkv_post_kernel,
        grid=(n // tm, RW // tc),
        in_specs=[pl.BlockSpec((2, tm, tc), lambda i, j: (0, i, j)), one, one, ch, ch,
                  pl.BlockSpec((tc, tc), lambda i, j: (0, 0))],
        out_specs=one,
        out_shape=jax.ShapeDtypeStruct((n, RW), BF16),
        compiler_params=_cparams(("parallel", "parallel")),
        name="rwkv_post",
    )(y, bonus, g, lnx_w, lnx_b, _head_block_ones(tc))


HALO = 16


FF_TN = 256
FF_TILES = D_FF // FF_TN


def _ffn_up_kernel(h_ref, hp_ref, hn_ref, wv_ref, wg_ref, cv, cg, bv, bg, o_ref, wvb_ref, wgb_ref, *, tm):
    j = pl.program_id(0)

    @pl.when(j < FF_TILES)
    def _():
        _cast_weight_once(wv_ref, wvb_ref)
        _cast_weight_once(wg_ref, wgb_ref)
        row0 = pl.program_id(1) * tm
        h = h_ref[...]
        edge = jnp.concatenate([hp_ref[...], hn_ref[...]], axis=0)

        def branch(wb_ref, c_ref, b_ref):
            w = wb_ref[...]
            u = jnp.dot(h, w, preferred_element_type=F32)
            ue = jnp.dot(edge, w, preferred_element_type=F32)
            up, un = _neighbours(u, ue[HALO - 1:HALO, :], ue[HALO:HALO + 1, :], row0)
            return up * c_ref[0:1, :] + u * c_ref[1:2, :] + un * c_ref[2:3, :] + b_ref[...]

        gate = branch(wgb_ref, cg, bg)
        act = gate * jax.nn.sigmoid(gate)
        o_ref[...] = (act * branch(wvb_ref, cv, bv)).astype(o_ref.dtype)

    @pl.when(j == FF_TILES)
    def _():
        o_ref[...] = jnp.zeros_like(o_ref)


def ffn_up_act(h, w_up, conv_w, conv_b, layer, tm=1024):
    n, k = h.shape
    tn = FF_TN
    nhb = n // HALO
    tb = tm // HALO

    def col(j):
        return jnp.minimum(j, FF_TILES - 1)

    def chan(rows, o):
        return pl.BlockSpec((None, rows, tn), lambda j, i: (layer, 0, o + col(j)))

    return pl.pallas_call(
        functools.partial(_ffn_up_kernel, tm=tm),
        grid=(D_FF_PAD // tn, n // tm),
        in_specs=[
            pl.BlockSpec((tm, k), lambda j, i: (i, 0)),
            pl.BlockSpec((HALO, k), lambda j, i: (jnp.maximum(i * tb - 1, 0), 0)),
            pl.BlockSpec((HALO, k), lambda j, i: (jnp.minimum((i + 1) * tb, nhb - 1), 0)),
            chan(k, 0), chan(k, FF_TILES),
            chan(3, 0), chan(3, FF_TILES), chan(1, 0), chan(1, FF_TILES),
        ],
        out_specs=pl.BlockSpec((tm, tn), lambda j, i: (i, j)),
        out_shape=jax.ShapeDtypeStruct((n, D_FF_PAD), BF16),
        scratch_shapes=[pltpu.VMEM((k, tn), BF16), pltpu.VMEM((k, tn), BF16)],
        compiler_params=_cparams(("parallel", "arbitrary")),
        name="ffn_up_act",
    )(h, h, h, w_up, w_up, conv_w, conv_w, conv_b, conv_b)


def _layer_params(l, norm_mix, q_gain, k_gain, rwkv_conv, decay_w0, decay_w2, iclr_a0, iclr_a2,
                  gate_g2, k_k, k_a, r_k, lnx_w, lnx_b, norm_ffn):
    gains = jnp.stack([q_gain[l] * (HEAD ** -0.5 * float(np.log2(np.e)))] * 4 + [k_gain[l]] * 2)[:, None, :]
    return dict(
        norm_mix=norm_mix[l], gains=gains, rwkv_conv=rwkv_conv[l],
        decay_w0=decay_w0[l], decay_w2=decay_w2[l].astype(BF16),
        iclr_a0=iclr_a0[l], iclr_a2=iclr_a2[l].astype(BF16),
        gate_g2=jnp.pad(gate_g2[l], ((0, GATE_LORA_PAD - GATE_LORA), (0, 0))).astype(BF16),
        k_k=k_k[l][None, :], k_a=k_a[l][None, :], r_k=r_k[l].reshape(1, RW),
        lnx_w=lnx_w[l][None, :], lnx_b=lnx_b[l][None, :], norm_ffn=norm_ffn[l],
    )


def _layer(x, l, p, big, tables, last):
    h = rmsnorm_cast(x, p["norm_mix"])
    qkv = qkv_proj(h, big["w_in_t"], l, p["gains"], tables)
    proj = matmul_nt(h, big["w_in_t"], l, C_GA - C_R, row0=C_R, name="in_proj")
    gates = gate_proj(h, big["w_in_t"], l)
    att = jnp.concatenate([attention(qkv, 0, P_BATCH, P_SEQ),
                           attention(qkv, N_PROMPT, S_BATCH, S_SEQ)], axis=0)
    r, v, kk, bonus, g, lw, kd, b = rwkv_prep(proj, p)
    y = rwkv_scan(r, v, kk, lw, kd, b)
    rw = rwkv_post(y, bonus, g, p["lnx_w"], p["lnx_b"])
    mixed = merge_branches(att, rw, big["w_up_attn"], big["w_up_rwkv"], l, gates)
    x = matmul(mixed, big["w_o"], l, D_MODEL, res=x, name="out_proj")
    h = rmsnorm_cast(x, p["norm_ffn"])
    act = ffn_up_act(h, big["w_ffn_up"], big["ffn_conv"], big["ffn_conv_b"], l)
    if not last:
        return matmul_ktiled_res(act, big["w_ffn_down"], l, x, name="ffn_down")
    return (matmul_ktiled_res(act, big["w_ffn_down"], l, x, row0=0, rows=N_PROMPT, name="ffn_down_prompt"),
            matmul_ktiled_res(act, big["w_ffn_down"], l, x, row0=N_PROMPT, rows=N_TOK - N_PROMPT,
                              name="ffn_down_sample"))


def kernel(x_prompt, x_sample, norm_mix, w_in, q_gain, k_gain, rwkv_conv, decay_w0, decay_w2, iclr_a0, iclr_a2, gate_g2, k_k, k_a, r_k, lnx_w, lnx_b, w_up_attn, w_up_rwkv, w_o, norm_ffn, w_ffn_up, ffn_conv, ffn_conv_b, w_ffn_down):
    small = (norm_mix, q_gain, k_gain, rwkv_conv, decay_w0, decay_w2, iclr_a0, iclr_a2,
             gate_g2, k_k, k_a, r_k, lnx_w, lnx_b, norm_ffn)
    big = dict(w_in_t=jnp.swapaxes(w_in, 1, 2),
               w_up_attn=w_up_attn, w_up_rwkv=w_up_rwkv, w_o=w_o,
               w_ffn_up=w_ffn_up, ffn_conv=ffn_conv, ffn_conv_b=ffn_conv_b[:, None, :], w_ffn_down=w_ffn_down)
    x = jnp.concatenate([x_prompt.reshape(N_PROMPT, D_MODEL), x_sample.reshape(-1, D_MODEL)], axis=0)
    tables = _rope_tables()
    depth = norm_mix.shape[0]
    for l in range(depth):
        x = _layer(x, l, _layer_params(l, *small), big, tables, last=l == depth - 1)
    y_prompt, y_sample = x
    return (y_prompt.reshape(P_BATCH, P_SEQ, D_MODEL), y_sample.reshape(S_BATCH, S_SEQ, D_MODEL))
```

```python
import functools

import jax
import jax.numpy as jnp
import numpy as np
from jax import lax
from jax.experimental import pallas as pl
from jax.experimental.pallas import tpu as pltpu

F32 = jnp.float32
BF16 = jnp.bfloat16

D_MODEL = 4096
P_BATCH, P_SEQ = 4, 2048
S_BATCH, S_SEQ = 2, 4096
N_PROMPT = P_BATCH * P_SEQ
N_TOK = N_PROMPT + S_BATCH * S_SEQ
GRID_W = 64
HEAD = 128
ATT_W = 2048
KV_W = 512
KV_HEADS = 4
GROUPS = 4
ROPE_THETA = 10000.0
RK = 64
RW = 2048
LORA = 128
GATE_LORA = 480
GATE_LORA_PAD = 512
D_FF = 11008
D_FF_PAD = 11264
NORM_EPS = 1e-6
LNX_EPS = RK * 1e-5

C_Q, C_K, C_V = 0, 2048, 2560
C_R, C_RK, C_RV = 3072, 5120, 7168
C_WL, C_AL, C_GL = 9216, 9472, 9728
C_GA = C_GL + GATE_LORA_PAD
IN_GATES = C_GL + GATE_LORA

CHUNK = 64
VMEM_LIMIT = 56 * 1024 * 1024


def _cparams(sem):
    return pltpu.CompilerParams(dimension_semantics=sem, vmem_limit_bytes=VMEM_LIMIT)


def _seq_edges(rows):
    pos = rows & jnp.where(rows < N_PROMPT, P_SEQ - 1, S_SEQ - 1)
    last = jnp.where(rows < N_PROMPT, P_SEQ - 1, S_SEQ - 1)
    return pos == 0, pos == last


def _neighbours(x, prev_row, next_row, row0):
    tm = x.shape[0]
    assert P_SEQ % tm == 0 and S_SEQ % tm == 0
    li = lax.broadcasted_iota(jnp.int32, x.shape, 0)
    starts, _ = _seq_edges(row0)
    _, ends = _seq_edges(row0 + (tm - 1))
    prev_row = jnp.where(starts, 0.0, prev_row)
    next_row = jnp.where(ends, 0.0, next_row)
    xp = jnp.where(li == 0, prev_row, pltpu.roll(x, 1, axis=0))
    xn = jnp.where(li == tm - 1, next_row, pltpu.roll(x, tm - 1, axis=0))
    return xp, xn


def _rmsnorm_kernel(x_ref, g_ref, o_ref):
    x = x_ref[...]
    ms = jnp.mean(x * x, axis=-1, keepdims=True)
    o_ref[...] = (x * lax.rsqrt(ms + NORM_EPS) * g_ref[...]).astype(o_ref.dtype)


def rmsnorm_cast(x, g, tm=256):
    n, d = x.shape
    return pl.pallas_call(
        _rmsnorm_kernel,
        grid=(n // tm,),
        in_specs=[pl.BlockSpec((tm, d), lambda i: (i, 0)), pl.BlockSpec((1, d), lambda i: (0, 0))],
        out_specs=pl.BlockSpec((tm, d), lambda i: (i, 0)),
        out_shape=jax.ShapeDtypeStruct((n, d), BF16),
        compiler_params=_cparams(("parallel",)),
        name="rmsnorm_cast",
    )(x, g.reshape(1, d))


def _cast_weight_once(w_ref, wb_ref):
    @pl.when(pl.program_id(1) == 0)
    def _():
        wb_ref[...] = w_ref[...].astype(BF16)


def _mm_kernel(a_ref, b_ref, o_ref, bb_ref):
    _cast_weight_once(b_ref, bb_ref)
    o_ref[...] = jnp.dot(a_ref[...], bb_ref[...], preferred_element_type=F32).astype(o_ref.dtype)


def _mm_res_kernel(a_ref, b_ref, r_ref, o_ref, bb_ref):
    _cast_weight_once(b_ref, bb_ref)
    o_ref[...] = r_ref[...] + jnp.dot(a_ref[...], bb_ref[...], preferred_element_type=F32)


def matmul(a, w, layer, n_cols, res=None, *, col0=0, tm=512, tn=512, name="matmul"):
    m, k = a.shape
    cb0 = col0 // tn
    in_specs = [pl.BlockSpec((tm, k), lambda j, i: (i, 0)),
                pl.BlockSpec((None, k, tn), lambda j, i: (layer, 0, cb0 + j))]
    args = [a, w]
    kern = _mm_kernel
    if res is not None:
        in_specs.append(pl.BlockSpec((tm, tn), lambda j, i: (i, j)))
        args.append(res)
        kern = _mm_res_kernel
    return pl.pallas_call(
        kern,
        grid=(n_cols // tn, m // tm),
        in_specs=in_specs,
        out_specs=pl.BlockSpec((tm, tn), lambda j, i: (i, j)),
        out_shape=jax.ShapeDtypeStruct((m, n_cols), F32),
        scratch_shapes=[pltpu.VMEM((k, tn), BF16)],
        compiler_params=_cparams(("parallel", "arbitrary")),
        name=name,
    )(*args)


def _dot_nt(a, wt):
    return lax.dot_general(a, wt, (((1,), (1,)), ((), ())), preferred_element_type=F32)


def _mm_nt_kernel(a_ref, w_ref, o_ref, wb_ref):
    _cast_weight_once(w_ref, wb_ref)
    o_ref[...] = _dot_nt(a_ref[...], wb_ref[...])


def matmul_nt(a, wt, layer, n_cols, *, row0=0, tm=512, tn=512, name="matmul_nt"):
    m, k = a.shape
    rb0 = row0 // tn
    return pl.pallas_call(
        _mm_nt_kernel,
        grid=(n_cols // tn, m // tm),
        in_specs=[pl.BlockSpec((tm, k), lambda j, i: (i, 0)),
                  pl.BlockSpec((None, tn, k), lambda j, i: (layer, rb0 + j, 0))],
        out_specs=pl.BlockSpec((tm, tn), lambda j, i: (i, j)),
        out_shape=jax.ShapeDtypeStruct((m, n_cols), F32),
        scratch_shapes=[pltpu.VMEM((tn, k), BF16)],
        compiler_params=_cparams(("parallel", "arbitrary")),
        name=name,
    )(a, wt)


def _gate_proj_kernel(a_ref, wl_ref, wr_ref, o_ref, wb_ref, *, shift):
    @pl.when(pl.program_id(1) == 0)
    def _():
        tn = wb_ref.shape[0]
        wb_ref[0:tn - shift, :] = wl_ref[shift:tn, :].astype(BF16)
        wb_ref[tn - shift:tn, :] = wr_ref[0:shift, :].astype(BF16)

    o_ref[...] = _dot_nt(a_ref[...], wb_ref[...])


def gate_proj(a, wt, layer, tm=512, tn=512):
    m, k = a.shape
    rb0, shift = IN_GATES // tn, IN_GATES % tn
    assert shift % 16 == 0 and (tn - shift) % 16 == 0
    return pl.pallas_call(
        functools.partial(_gate_proj_kernel, shift=shift),
        grid=(2 * D_MODEL // tn, m // tm),
        in_specs=[pl.BlockSpec((tm, k), lambda j, i: (i, 0)),
                  pl.BlockSpec((None, tn, k), lambda j, i: (layer, rb0 + j, 0)),
                  pl.BlockSpec((None, tn, k), lambda j, i: (layer, rb0 + j + 1, 0))],
        out_specs=pl.BlockSpec((tm, tn), lambda j, i: (i, j)),
        out_shape=jax.ShapeDtypeStruct((m, 2 * D_MODEL), F32),
        scratch_shapes=[pltpu.VMEM((tn, k), BF16)],
        compiler_params=_cparams(("parallel", "arbitrary")),
        name="gate_proj",
    )(a, wt, wt)


def _mmk_res_kernel(a_ref, b_ref, r_ref, o_ref, acc_ref, *, k_rows):
    kk = pl.program_id(2)
    tk = b_ref.shape[0]

    @pl.when(kk == 0)
    def _():
        acc_ref[...] = r_ref[...]

    row = lax.broadcasted_iota(jnp.int32, b_ref.shape, 0)
    b = jnp.where(row < k_rows - kk * tk, b_ref[...], 0.0).astype(BF16)
    acc_ref[...] += jnp.dot(a_ref[...], b, preferred_element_type=F32)

    @pl.when(kk == pl.num_programs(2) - 1)
    def _():
        o_ref[...] = acc_ref[...]


def matmul_ktiled_res(a, w, layer, res, *, row0=0, rows=None, tm=1024, tn=1024, tk=1408, name="matmul_k"):
    k = a.shape[1]
    m = a.shape[0] if rows is None else rows
    n = w.shape[2]
    rb0 = row0 // tm
    return pl.pallas_call(
        functools.partial(_mmk_res_kernel, k_rows=w.shape[1]),
        grid=(m // tm, n // tn, k // tk),
        in_specs=[
            pl.BlockSpec((tm, tk), lambda i, j, kk: (rb0 + i, kk)),
            pl.BlockSpec((None, tk, tn), lambda i, j, kk: (layer, kk, j)),
            pl.BlockSpec((tm, tn), lambda i, j, kk: (rb0 + i, j)),
        ],
        out_specs=pl.BlockSpec((tm, tn), lambda i, j, kk: (i, j)),
        out_shape=jax.ShapeDtypeStruct((m, n), F32),
        scratch_shapes=[pltpu.VMEM((tm, tn), F32)],
        compiler_params=_cparams(("parallel", "parallel", "arbitrary")),
        name=name,
    )(a, w, res)


def _merge_kernel(att_ref, rw_ref, wa_ref, wr_ref, ga_ref, gr_ref, o_ref, wab_ref, wrb_ref):
    _cast_weight_once(wa_ref, wab_ref)
    _cast_weight_once(wr_ref, wrb_ref)
    ya = jnp.dot(att_ref[...], wab_ref[...], preferred_element_type=F32)
    yr = jnp.dot(rw_ref[...], wrb_ref[...], preferred_element_type=F32)
    o_ref[...] = (jax.nn.sigmoid(ga_ref[...]) * ya + jax.nn.sigmoid(gr_ref[...]) * yr).astype(o_ref.dtype)


def merge_branches(att, rw, wa, wr, layer, gates, tm=512, tn=512):
    n = att.shape[0]
    gr0 = D_MODEL // tn
    return pl.pallas_call(
        _merge_kernel,
        grid=(D_MODEL // tn, n // tm),
        in_specs=[
            pl.BlockSpec((tm, ATT_W), lambda j, i: (i, 0)),
            pl.BlockSpec((tm, RW), lambda j, i: (i, 0)),
            pl.BlockSpec((None, ATT_W, tn), lambda j, i: (layer, 0, j)),
            pl.BlockSpec((None, RW, tn), lambda j, i: (layer, 0, j)),
            pl.BlockSpec((tm, tn), lambda j, i: (i, j)),
            pl.BlockSpec((tm, tn), lambda j, i: (i, gr0 + j)),
        ],
        out_specs=pl.BlockSpec((tm, tn), lambda j, i: (i, j)),
        out_shape=jax.ShapeDtypeStruct((n, D_MODEL), BF16),
        scratch_shapes=[pltpu.VMEM((ATT_W, tn), BF16), pltpu.VMEM((RW, tn), BF16)],
        compiler_params=_cparams(("parallel", "arbitrary")),
        name="merge_branches",
    )(att, rw, wa, wr, gates, gates)


def _qkv_proj_kernel(a_ref, w_ref, gain_ref, cos_ref, sa_ref, sb_ref, o_ref, wb_ref):
    _cast_weight_once(w_ref, wb_ref)
    acc = _dot_nt(a_ref[...], wb_ref[...])
    j = pl.program_id(0)

    @pl.when(j < 5)
    def _():
        cos, sa, sb = cos_ref[...], sa_ref[...], sb_ref[...]
        gain = gain_ref[0]
        for h in range(4):
            x = acc[:, h * HEAD:(h + 1) * HEAD]
            ms = jnp.mean(x * x, axis=-1, keepdims=True)
            y = x * lax.rsqrt(ms + NORM_EPS) * gain
            y = y * cos + pltpu.roll(y, 96, axis=1) * sa + pltpu.roll(y, 32, axis=1) * sb
            o_ref[:, h * HEAD:(h + 1) * HEAD] = y.astype(o_ref.dtype)

    @pl.when(j == 5)
    def _():
        o_ref[...] = acc.astype(o_ref.dtype)


def _rope_tables():
    t = np.arange(S_SEQ)
    row = (t // GRID_W).astype(np.float32)
    col = (t % GRID_W).astype(np.float32)
    axis_dim = HEAD // 2
    inv = jnp.asarray(ROPE_THETA, F32) ** (-jnp.arange(0, axis_dim, 2, dtype=F32) / axis_dim)
    ang_r = jnp.asarray(row)[:, None] * inv
    ang_c = jnp.asarray(col)[:, None] * inv
    cr, sr, cc, sc = jnp.cos(ang_r), jnp.sin(ang_r), jnp.cos(ang_c), jnp.sin(ang_c)
    z = jnp.zeros_like(sr)
    cos = jnp.concatenate([cr, cr, cc, cc], axis=1)
    sin_a = jnp.concatenate([-sr, z, -sc, z], axis=1)
    sin_b = jnp.concatenate([z, sr, z, sc], axis=1)
    return cos, sin_a, sin_b


def qkv_proj(h, w_in, layer, gains, tables, tm=512, tn=512):
    n, k = h.shape
    npb = N_PROMPT // tm

    def tab_map(j, i):
        return (jnp.where(i < npb, i % (P_SEQ // tm), (i - npb) % (S_SEQ // tm)), 0)

    tab_spec = pl.BlockSpec((tm, HEAD), tab_map)
    return pl.pallas_call(
        _qkv_proj_kernel,
        grid=(C_R // tn, n // tm),
        in_specs=[
            pl.BlockSpec((tm, k), lambda j, i: (i, 0)),
            pl.BlockSpec((None, tn, k), lambda j, i: (layer, j, 0)),
            pl.BlockSpec((1, 1, HEAD), lambda j, i: (j, 0, 0)),
            tab_spec, tab_spec, tab_spec,
        ],
        out_specs=pl.BlockSpec((tm, tn), lambda j, i: (i, j)),
        out_shape=jax.ShapeDtypeStruct((n, C_R), BF16),
        scratch_shapes=[pltpu.VMEM((tn, k), BF16)],
        compiler_params=_cparams(("parallel", "arbitrary")),
        name="qkv_proj",
    )(h, w_in, gains, *tables)


def _attn_kernel(q_ref, k_ref, v_ref, o_ref, v1_ref):
    @pl.when(pl.program_id(2) == 0)
    def _():
        v1_ref[:, 0:HEAD] = v_ref[...]
        v1_ref[:, HEAD:2 * HEAD] = jnp.ones(v_ref.shape, v1_ref.dtype)

    k = k_ref[...]
    v1 = v1_ref[...]
    for g in range(GROUPS):
        q = q_ref[:, g * HEAD:(g + 1) * HEAD]
        s = lax.dot_general(q, k, (((1,), (1,)), ((), ())), preferred_element_type=F32)
        p = jnp.exp2(s - jnp.max(s, axis=-1, keepdims=True))
        ol = jnp.dot(p.astype(BF16), v1, preferred_element_type=F32)
        o_ref[:, g * HEAD:(g + 1) * HEAD] = (ol[:, 0:HEAD] / ol[:, HEAD:2 * HEAD]).astype(o_ref.dtype)


def attention(qkv, row0, batch, seq, tq=512):
    qb0 = row0 // tq
    sb0 = row0 // seq
    nq = seq // tq
    return pl.pallas_call(
        _attn_kernel,
        grid=(batch, KV_HEADS, nq),
        in_specs=[
            pl.BlockSpec((tq, GROUPS * HEAD), lambda b, h, i: (qb0 + b * nq + i, h)),
            pl.BlockSpec((seq, HEAD), lambda b, h, i: (sb0 + b, C_K // HEAD + h)),
            pl.BlockSpec((seq, HEAD), lambda b, h, i: (sb0 + b, C_V // HEAD + h)),
        ],
        out_specs=pl.BlockSpec((tq, GROUPS * HEAD), lambda b, h, i: (b * nq + i, h)),
        out_shape=jax.ShapeDtypeStruct((batch * seq, ATT_W), BF16),
        scratch_shapes=[pltpu.VMEM((seq, 2 * HEAD), BF16)],
        compiler_params=_cparams(("parallel", "parallel", "arbitrary")),
        name="attention",
    )(qkv, qkv, qkv)


def _rwkv_prep_kernel(xr, xr_p, xr_n, xk, xk_p, xk_n, xv, xv_p, xv_n, wl_ref, al_ref, gl_ref,
                      cr, ck, cv, w0, w2, a0, a2, g2, kkw, kaw, rkw, bd_ref,
                      r_o, v_o, kk_o, bonus_o, g_o, lw_o, kd_o, b_o, *, tm):
    row0 = pl.program_id(0) * tm

    def conv(x_ref, p_ref, n_ref, c_ref):
        x = x_ref[...]
        xp, xn = _neighbours(x, p_ref[7:8, :], n_ref[0:1, :], row0)
        return xp * c_ref[0:1, :] + x * c_ref[1:2, :] + xn * c_ref[2:3, :]

    r = conv(xr, xr_p, xr_n, cr)
    k = conv(xk, xk_p, xk_n, ck)
    v = conv(xv, xv_p, xv_n, cv)
    bd = bd_ref[...]

    def segsum(z):
        return _head_sum(z, bd)

    kkr = k * kkw[...]
    kk = kkr / jnp.maximum(jnp.sqrt(segsum(kkr * kkr)), 1e-12)
    g = jnp.dot(jax.nn.sigmoid(gl_ref[...]).astype(BF16), g2[...], preferred_element_type=F32)
    r_o[...] = r
    v_o[...] = v.astype(v_o.dtype)
    kk_o[...] = kk
    g_o[...] = g
    bc = jnp.zeros_like(r)
    for d in range(2):
        wl = jnp.tanh(wl_ref[:, d * LORA:(d + 1) * LORA]).astype(BF16)
        w_raw = w0[d:d + 1, :] + jnp.dot(wl, w2[d], preferred_element_type=F32)
        lw_o[d] = jax.nn.sigmoid(w_raw) * (-float(np.exp(-0.5)))
        al = al_ref[:, d * LORA:(d + 1) * LORA].astype(BF16)
        a = jax.nn.sigmoid(a0[d:d + 1, :] + jnp.dot(al, a2[d], preferred_element_type=F32))
        kd = k * (1.0 + (a - 1.0) * kaw[...])
        kd_o[d] = kd
        b_o[d] = a * kk
        bc = bc + r * kd * rkw[...]
    bonus_o[...] = segsum(bc) * v


def _head_block_ones(width):
    i = np.arange(width) // RK
    return jnp.asarray((i[:, None] == i[None, :]).astype(np.float32)).astype(BF16)


def _head_sum(z, ones_bd):
    hi = z.astype(BF16)
    lo = (z - hi.astype(F32)).astype(BF16)
    return (jnp.dot(hi, ones_bd, preferred_element_type=F32)
            + jnp.dot(lo, ones_bd, preferred_element_type=F32))


def rwkv_prep(proj, p, tm=256, tc=512):
    n = proj.shape[0]
    nrb = n // 8
    tb = tm // 8

    def main(c0):
        return pl.BlockSpec((tm, tc), lambda i, j: (i, c0 // tc + j))

    def prev(c0):
        return pl.BlockSpec((8, tc), lambda i, j: (jnp.maximum(i * tb - 1, 0), c0 // tc + j))

    def nxt(c0):
        return pl.BlockSpec((8, tc), lambda i, j: (jnp.minimum((i + 1) * tb, nrb - 1), c0 // tc + j))

    def chan(rows, c0=0):
        return pl.BlockSpec((rows, tc), lambda i, j: (0, c0 // tc + j))

    in_specs = []
    for c0 in (C_R, C_RK, C_RV):
        in_specs += [main(c0 - C_R), prev(c0 - C_R), nxt(c0 - C_R)]
    in_specs += [
        pl.BlockSpec((tm, 2 * LORA), lambda i, j: (i, (C_WL - C_R) // (2 * LORA))),
        pl.BlockSpec((tm, 2 * LORA), lambda i, j: (i, (C_AL - C_R) // (2 * LORA))),
        pl.BlockSpec((tm, GATE_LORA_PAD), lambda i, j: (i, (C_GL - C_R) // GATE_LORA_PAD)),
        chan(3, 0), chan(3, RW), chan(3, 2 * RW),
        chan(2),
        pl.BlockSpec((2, LORA, tc), lambda i, j: (0, 0, j)),
        chan(2),
        pl.BlockSpec((2, LORA, tc), lambda i, j: (0, 0, j)),
        pl.BlockSpec((GATE_LORA_PAD, tc), lambda i, j: (0, j)),
        chan(1), chan(1), chan(1),
        pl.BlockSpec((tc, tc), lambda i, j: (0, 0)),
    ]
    one = pl.BlockSpec((tm, tc), lambda i, j: (i, j))
    two = pl.BlockSpec((2, tm, tc), lambda i, j: (0, i, j))
    s1 = jax.ShapeDtypeStruct((n, RW), F32)
    s2 = jax.ShapeDtypeStruct((2, n, RW), F32)
    return pl.pallas_call(
        functools.partial(_rwkv_prep_kernel, tm=tm),
        grid=(n // tm, RW // tc),
        in_specs=in_specs,
        out_specs=[one, one, one, one, one, two, two, two],
        out_shape=[s1, jax.ShapeDtypeStruct((n, RW), BF16), s1, s1, s1, s2, s2, s2],
        compiler_params=_cparams(("parallel", "parallel")),
        name="rwkv_prep",
    )(proj, proj, proj, proj, proj, proj, proj, proj, proj, proj, proj, proj,
      p["rwkv_conv"], p["rwkv_conv"], p["rwkv_conv"], p["decay_w0"], p["decay_w2"], p["iclr_a0"],
      p["iclr_a2"], p["gate_g2"], p["k_k"], p["k_a"], p["r_k"], _head_block_ones(tc))


def _scan_kernel(r_ref, v_ref, kk_ref, lw_ref, kd_ref, b_ref, y_ref, s_ref, *, pg, nc):
    d = pl.program_id(0)
    c = pl.program_id(2)
    chunk = c + d * (nc - 1 - 2 * c)
    npc = N_PROMPT // CHUNK
    in_p = chunk < npc
    per = jnp.where(in_p, P_SEQ // CHUNK, S_SEQ // CHUNK)
    rel = jnp.where(in_p, chunk, chunk - npc)
    @pl.when((rel + d) % per == 0)
    def _():
        s_ref[...] = jnp.zeros_like(s_ref)

    sgn = 1 - 2 * d
    C = CHUNK
    t_i = lax.broadcasted_iota(jnp.int32, (C, 2 * C), 0)
    s_i = lax.broadcasted_iota(jnp.int32, (C, 2 * C), 1) & (C - 1)
    rel_ts = (t_i - s_i) * sgn
    strict = rel_ts > 0
    incl = rel_ts >= 0
    lane = lax.broadcasted_iota(jnp.int32, (C, 2 * C), 1)
    m_a = lane < C
    tt = lax.broadcasted_iota(jnp.int32, (C, C), 0)
    ss = lax.broadcasted_iota(jnp.int32, (C, C), 1)
    tri = jnp.where((tt - ss) * sgn >= 0, 1.0, 0.0).astype(BF16)
    ri = lax.broadcasted_iota(jnp.int32, (2 * C, 2 * C), 0)
    ci = lax.broadcasted_iota(jnp.int32, (2 * C, 2 * C), 1)
    same_head = (ri >> 6) == (ci >> 6)
    eye = jnp.where(ri == ci, 1.0, 0.0).astype(F32)
    same8 = (ri >> 3) == (ci >> 3)
    lvl = [((ri >> (s + 1)) == (ci >> (s + 1))) & ((ri >> s) != (ci >> s)) for s in (3, 4, 5)]

    def split(x):
        return jnp.concatenate([jnp.where(m_a, x, 0.0), jnp.where(m_a, 0.0, x)], axis=0)

    def bf(x):
        return x.astype(BF16)

    def mm(a, b):
        return jnp.dot(a, b, preferred_element_type=F32)

    def mm_nt(a, b):
        return lax.dot_general(a, b, (((1,), (1,)), ((), ())), preferred_element_type=F32)

    def mm_tn(a, b):
        return lax.dot_general(a, b, (((0,), (0,)), ((), ())), preferred_element_type=F32)

    def cumsum(lw):
        hi = bf(lw)
        r1 = lw - hi.astype(F32)
        mid = bf(r1)
        lo = bf(r1 - mid.astype(F32))
        return mm(tri, hi) + (mm(tri, mid) + mm(tri, lo))

    P = range(pg)
    sls = [slice(p * 2 * C, (p + 1) * 2 * C) for p in P]
    r = [r_ref[:, sl] for sl in sls]
    v = [v_ref[:, sl] for sl in sls]
    kk = [kk_ref[:, sl] for sl in sls]
    lw = [lw_ref[0, :, sl] for sl in sls]
    kd = [kd_ref[0, :, sl] for sl in sls]
    b = [b_ref[0, :, sl] for sl in sls]
    cl = [cumsum(lw[p]) for p in P]
    tot = [jnp.sum(lw[p], axis=0, keepdims=True) for p in P]
    e_neg = [jnp.exp(-cl[p]) for p in P]
    e_end = [jnp.exp(tot[p] - cl[p]) for p in P]
    x = [bf(jnp.concatenate([r[p] * jnp.exp(cl[p]), kk[p] * jnp.exp(cl[p] - lw[p])], axis=0)) for p in P]
    z = [bf(jnp.concatenate([split(kd[p] * e_neg[p]), split(b[p] * e_neg[p])], axis=0)) for p in P]
    pm = [mm_nt(x[p], z[p]) for p in P]
    ld = [split(jnp.where(strict, pm[p][C:2 * C, 2 * C:4 * C], 0.0)) for p in P]
    l8 = [jnp.where(same8, ld[p], 0.0) for p in P]
    l8b = [bf(l8[p]) for p in P]
    l8_2 = [mm(l8b[p], l8b[p]) for p in P]
    l8_2b = [bf(l8_2[p]) for p in P]
    l8_4 = [mm(l8_2b[p], l8_2b[p]) for p in P]
    t1 = [mm(bf(eye - l8[p]), bf(eye + l8_2[p])) for p in P]
    inv = [mm(bf(t1[p]), bf(eye + l8_4[p])) for p in P]
    for msk in lvl:
        invb = [bf(inv[p]) for p in P]
        t2 = [mm(invb[p], bf(jnp.where(msk, ld[p], 0.0))) for p in P]
        inv = [inv[p] - mm(bf(t2[p]), invb[p]) for p in P]
    invb = [bf(inv[p]) for p in P]
    st = [s_ref[p] for p in P]
    xs = [mm_nt(x[p], bf(st[p])) for p in P]
    vs = [bf(split(v[p])) for p in P]
    rhs = [xs[p][C:2 * C] + mm(bf(jnp.where(strict, pm[p][C:2 * C, 0:2 * C], 0.0)), vs[p]) for p in P]
    ud = [mm(invb[p], bf(split(rhs[p]))) for p in P]
    for p in P:
        coef = jnp.concatenate([jnp.where(incl, pm[p][0:C, 0:2 * C], 0.0),
                                jnp.where(incl, -pm[p][0:C, 2 * C:4 * C], 0.0)], axis=1)
        y_ref[0, :, sls[p]] = xs[p][0:C] + mm(bf(coef), jnp.concatenate([vs[p], bf(ud[p])], axis=0))
    for p in P:
        u = ud[p][0:C] + ud[p][C:2 * C]
        upd = mm_tn(jnp.concatenate([v[p], bf(u)], axis=0),
                    bf(jnp.concatenate([kd[p] * e_end[p], -(b[p] * e_end[p])], axis=0)))
        s_ref[p] = st[p] * jnp.exp(tot[p]) + jnp.where(same_head, upd, 0.0)


def rwkv_scan(r, v, kk, lw, kd, b, pg=16):
    n = r.shape[0]
    nc = n // CHUNK
    w = pg * 2 * CHUNK

    def cmap(d, g, c):
        return (c + d * (nc - 1 - 2 * c), g)

    def dmap(d, g, c):
        return (d, c + d * (nc - 1 - 2 * c), g)

    one = pl.BlockSpec((CHUNK, w), cmap)
    two = pl.BlockSpec((1, CHUNK, w), dmap)
    return pl.pallas_call(
        functools.partial(_scan_kernel, pg=pg, nc=nc),
        grid=(2, RW // w, nc),
        in_specs=[one, one, one, two, two, two],
        out_specs=two,
        out_shape=jax.ShapeDtypeStruct((2, n, RW), F32),
        scratch_shapes=[pltpu.VMEM((pg, 2 * CHUNK, 2 * CHUNK), F32)],
        compiler_params=_cparams(("parallel", "parallel", "arbitrary")),
        name="rwkv_scan",
    )(r, v, kk, lw, kd, b)


def _rwkv_post_kernel(y_ref, bonus_ref, g_ref, w_ref, b_ref, bd_ref, o_ref):
    bd = bd_ref[...]

    def segmean(z):
        return _head_sum(z, bd) * (1.0 / RK)

    y = y_ref[0] + y_ref[1]
    yc = y - segmean(y)
    var = segmean(yc * yc)
    out = yc * lax.rsqrt(var + LNX_EPS) * w_ref[...] + b_ref[...]
    o_ref[...] = ((out + bonus_ref[...]) * g_ref[...]).astype(o_ref.dtype)


def rwkv_post(y, bonus, g, lnx_w, lnx_b, tm=256, tc=512):
    n = bonus.shape[0]
    one = pl.BlockSpec((tm, tc), lambda i, j: (i, j))
    ch = pl.BlockSpec((1, tc), lambda i, j: (0, j))
    return pl.pallas_call(
        _rwkv_post_kernel,
        grid=(n // tm, RW // tc),
        in_specs=[pl.BlockSpec((2, tm, tc), lambda i, j: (0, i, j)), one, one, ch, ch,
                  pl.BlockSpec((tc, tc), lambda i, j: (0, 0))],
        out_specs=one,
        out_shape=jax.ShapeDtypeStruct((n, RW), BF16),
        compiler_params=_cparams(("parallel", "parallel")),
        name="rwkv_post",
    )(y, bonus, g, lnx_w, lnx_b, _head_block_ones(tc))


HALO = 16


FF_TN = 256
FF_TILES = D_FF // FF_TN


def _ffn_up_kernel(h_ref, hp_ref, hn_ref, wv_ref, wg_ref, cv, cg, bv, bg, o_ref, wvb_ref, wgb_ref, *, tm):
    j = pl.program_id(0)

    @pl.when(j < FF_TILES)
    def _():
        _cast_weight_once(wv_ref, wvb_ref)
        _cast_weight_once(wg_ref, wgb_ref)
        row0 = pl.program_id(1) * tm
        h = h_ref[...]
        edge = jnp.concatenate([hp_ref[...], hn_ref[...]], axis=0)

        def branch(wb_ref, c_ref, b_ref):
            w = wb_ref[...]
            u = jnp.dot(h, w, preferred_element_type=F32)
            ue = jnp.dot(edge, w, preferred_element_type=F32)
            up, un = _neighbours(u, ue[HALO - 1:HALO, :], ue[HALO:HALO + 1, :], row0)
            return up * c_ref[0:1, :] + u * c_ref[1:2, :] + un * c_ref[2:3, :] + b_ref[...]

        gate = branch(wgb_ref, cg, bg)
        act = gate * jax.nn.sigmoid(gate)
        o_ref[...] = (act * branch(wvb_ref, cv, bv)).astype(o_ref.dtype)

    @pl.when(j == FF_TILES)
    def _():
        o_ref[...] = jnp.zeros_like(o_ref)


def ffn_up_act(h, w_up, conv_w, conv_b, layer, tm=1024):
    n, k = h.shape
    tn = FF_TN
    nhb = n // HALO
    tb = tm // HALO

    def col(j):
        return jnp.minimum(j, FF_TILES - 1)

    def chan(rows, o):
        return pl.BlockSpec((None, rows, tn), lambda j, i: (layer, 0, o + col(j)))

    return pl.pallas_call(
        functools.partial(_ffn_up_kernel, tm=tm),
        grid=(D_FF_PAD // tn, n // tm),
        in_specs=[
            pl.BlockSpec((tm, k), lambda j, i: (i, 0)),
            pl.BlockSpec((HALO, k), lambda j, i: (jnp.maximum(i * tb - 1, 0), 0)),
            pl.BlockSpec((HALO, k), lambda j, i: (jnp.minimum((i + 1) * tb, nhb - 1), 0)),
            chan(k, 0), chan(k, FF_TILES),
            chan(3, 0), chan(3, FF_TILES), chan(1, 0), chan(1, FF_TILES),
        ],
        out_specs=pl.BlockSpec((tm, tn), lambda j, i: (i, j)),
        out_shape=jax.ShapeDtypeStruct((n, D_FF_PAD), BF16),
        scratch_shapes=[pltpu.VMEM((k, tn), BF16), pltpu.VMEM((k, tn), BF16)],
        compiler_params=_cparams(("parallel", "arbitrary")),
        name="ffn_up_act",
    )(h, h, h, w_up, w_up, conv_w, conv_w, conv_b, conv_b)


def _layer_params(l, norm_mix, q_gain, k_gain, rwkv_conv, decay_w0, decay_w2, iclr_a0, iclr_a2,
                  gate_g2, k_k, k_a, r_k, lnx_w, lnx_b, norm_ffn):
    gains = jnp.stack([q_gain[l] * (HEAD ** -0.5 * float(np.log2(np.e)))] * 4 + [k_gain[l]] * 2)[:, None, :]
    return dict(
        norm_mix=norm_mix[l], gains=gains, rwkv_conv=rwkv_conv[l],
        decay_w0=decay_w0[l], decay_w2=decay_w2[l].astype(BF16),
        iclr_a0=iclr_a0[l], iclr_a2=iclr_a2[l].astype(BF16),
        gate_g2=jnp.pad(gate_g2[l], ((0, GATE_LORA_PAD - GATE_LORA), (0, 0))).astype(BF16),
        k_k=k_k[l][None, :], k_a=k_a[l][None, :], r_k=r_k[l].reshape(1, RW),
        lnx_w=lnx_w[l][None, :], lnx_b=lnx_b[l][None, :], norm_ffn=norm_ffn[l],
    )


def _layer(x, l, p, big, tables, last):
    h = rmsnorm_cast(x, p["norm_mix"])
    qkv = qkv_proj(h, big["w_in_t"], l, p["gains"], tables)
    proj = matmul_nt(h, big["w_in_t"], l, C_GA - C_R, row0=C_R, name="in_proj")
    gates = gate_proj(h, big["w_in_t"], l)
    att = jnp.concatenate([attention(qkv, 0, P_BATCH, P_SEQ),
                           attention(qkv, N_PROMPT, S_BATCH, S_SEQ)], axis=0)
    r, v, kk, bonus, g, lw, kd, b = rwkv_prep(proj, p)
    y = rwkv_scan(r, v, kk, lw, kd, b)
    rw = rwkv_post(y, bonus, g, p["lnx_w"], p["lnx_b"])
    mixed = merge_branches(att, rw, big["w_up_attn"], big["w_up_rwkv"], l, gates)
    x = matmul(mixed, big["w_o"], l, D_MODEL, res=x, name="out_proj")
    h = rmsnorm_cast(x, p["norm_ffn"])
    act = ffn_up_act(h, big["w_ffn_up"], big["ffn_conv"], big["ffn_conv_b"], l)
    if not last:
        return matmul_ktiled_res(act, big["w_ffn_down"], l, x, name="ffn_down")
    return (matmul_ktiled_res(act, big["w_ffn_down"], l, x, row0=0, rows=N_PROMPT, name="ffn_down_prompt"),
            matmul_ktiled_res(act, big["w_ffn_down"], l, x, row0=N_PROMPT, rows=N_TOK - N_PROMPT,
                              name="ffn_down_sample"))


def kernel(x_prompt, x_sample, norm_mix, w_in, q_gain, k_gain, rwkv_conv, decay_w0, decay_w2, iclr_a0, iclr_a2, gate_g2, k_k, k_a, r_k, lnx_w, lnx_b, w_up_attn, w_up_rwkv, w_o, norm_ffn, w_ffn_up, ffn_conv, ffn_conv_b, w_ffn_down):
    small = (norm_mix, q_gain, k_gain, rwkv_conv, decay_w0, decay_w2, iclr_a0, iclr_a2,
             gate_g2, k_k, k_a, r_k, lnx_w, lnx_b, norm_ffn)
    big = dict(w_in_t=jnp.swapaxes(w_in, 1, 2),
               w_up_attn=w_up_attn, w_up_rwkv=w_up_rwkv, w_o=w_o,
               w_ffn_up=w_ffn_up, ffn_conv=ffn_conv, ffn_conv_b=ffn_conv_b[:, None, :], w_ffn_down=w_ffn_down)
    x = jnp.concatenate([x_prompt.reshape(N_PROMPT, D_MODEL), x_sample.reshape(-1, D_MODEL)], axis=0)
    tables = _rope_tables()
    depth = norm_mix.shape[0]
    for l in range(depth):
        x = _layer(x, l, _layer_params(l, *small), big, tables, last=l == depth - 1)
    y_prompt, y_sample = x
    return (y_prompt.reshape(P_BATCH, P_SEQ, D_MODEL), y_sample.reshape(S_BATCH, S_SEQ, D_MODEL))
```

```python
import functools

import jax
import jax.numpy as jnp
import numpy as np
from jax import lax
from jax.experimental import pallas as pl
from jax.experimental.pallas import tpu as pltpu

F32 = jnp.float32
BF16 = jnp.bfloat16

D_MODEL = 4096
P_BATCH, P_SEQ = 4, 2048
S_BATCH, S_SEQ = 2, 4096
N_PROMPT = P_BATCH * P_SEQ
N_TOK = N_PROMPT + S_BATCH * S_SEQ
GRID_W = 64
HEAD = 128
ATT_W = 2048
KV_W = 512
KV_HEADS = 4
GROUPS = 4
ROPE_THETA = 10000.0
RK = 64
RW = 2048
LORA = 128
GATE_LORA = 480
GATE_LORA_PAD = 512
D_FF = 11008
D_FF_PAD = 11264
NORM_EPS = 1e-6
LNX_EPS = RK * 1e-5

C_Q, C_K, C_V = 0, 2048, 2560
C_R, C_RK, C_RV = 3072, 5120, 7168
C_WL, C_AL, C_GL = 9216, 9472, 9728
C_GA = C_GL + GATE_LORA_PAD
IN_GATES = C_GL + GATE_LORA

CHUNK = 64
VMEM_LIMIT = 56 * 1024 * 1024


def _cparams(sem):
    return pltpu.CompilerParams(dimension_semantics=sem, vmem_limit_bytes=VMEM_LIMIT)


def _seq_edges(rows):
    pos = rows & jnp.where(rows < N_PROMPT, P_SEQ - 1, S_SEQ - 1)
    last = jnp.where(rows < N_PROMPT, P_SEQ - 1, S_SEQ - 1)
    return pos == 0, pos == last


def _neighbours(x, prev_row, next_row, row0):
    tm = x.shape[0]
    assert P_SEQ % tm == 0 and S_SEQ % tm == 0
    li = lax.broadcasted_iota(jnp.int32, x.shape, 0)
    starts, _ = _seq_edges(row0)
    _, ends = _seq_edges(row0 + (tm - 1))
    prev_row = jnp.where(starts, 0.0, prev_row)
    next_row = jnp.where(ends, 0.0, next_row)
    xp = jnp.where(li == 0, prev_row, pltpu.roll(x, 1, axis=0))
    xn = jnp.where(li == tm - 1, next_row, pltpu.roll(x, tm - 1, axis=0))
    return xp, xn


def _rmsnorm_kernel(x_ref, g_ref, o_ref):
    x = x_ref[...]
    ms = jnp.mean(x * x, axis=-1, keepdims=True)
    o_ref[...] = (x * lax.rsqrt(ms + NORM_EPS) * g_ref[...]).astype(o_ref.dtype)


def rmsnorm_cast(x, g, tm=256):
    n, d = x.shape
    return pl.pallas_call(
        _rmsnorm_kernel,
        grid=(n // tm,),
        in_specs=[pl.BlockSpec((tm, d), lambda i: (i, 0)), pl.BlockSpec((1, d), lambda i: (0, 0))],
        out_specs=pl.BlockSpec((tm, d), lambda i: (i, 0)),
        out_shape=jax.ShapeDtypeStruct((n, d), BF16),
        compiler_params=_cparams(("parallel",)),
        name="rmsnorm_cast",
    )(x, g.reshape(1, d))


def _cast_weight_once(w_ref, wb_ref):
    @pl.when(pl.program_id(1) == 0)
    def _():
        wb_ref[...] = w_ref[...].astype(BF16)


def _mm_kernel(a_ref, b_ref, o_ref, bb_ref):
    _cast_weight_once(b_ref, bb_ref)
    o_ref[...] = jnp.dot(a_ref[...], bb_ref[...], preferred_element_type=F32).astype(o_ref.dtype)


def _mm_res_kernel(a_ref, b_ref, r_ref, o_ref, bb_ref):
    _cast_weight_once(b_ref, bb_ref)
    o_ref[...] = r_ref[...] + jnp.dot(a_ref[...], bb_ref[...], preferred_element_type=F32)


def matmul(a, w, layer, n_cols, res=None, *, col0=0, tm=1024, tn=512, name="matmul"):
    m, k = a.shape
    cb0 = col0 // tn
    in_specs = [pl.BlockSpec((tm, k), lambda j, i: (i, 0)),
                pl.BlockSpec((None, k, tn), lambda j, i: (layer, 0, cb0 + j))]
    args = [a, w]
    kern = _mm_kernel
    if res is not None:
        in_specs.append(pl.BlockSpec((tm, tn), lambda j, i: (i, j)))
        args.append(res)
        kern = _mm_res_kernel
    return pl.pallas_call(
        kern,
        grid=(n_cols // tn, m // tm),
        in_specs=in_specs,
        out_specs=pl.BlockSpec((tm, tn), lambda j, i: (i, j)),
        out_shape=jax.ShapeDtypeStruct((m, n_cols), F32),
        scratch_shapes=[pltpu.VMEM((k, tn), BF16)],
        compiler_params=_cparams(("parallel", "arbitrary")),
        name=name,
    )(*args)


def _dot_nt(a, wt):
    return lax.dot_general(a, wt, (((1,), (1,)), ((), ())), preferred_element_type=F32)


def _mm_nt_kernel(a_ref, w_ref, o_ref, wb_ref):
    _cast_weight_once(w_ref, wb_ref)
    o_ref[...] = _dot_nt(a_ref[...], wb_ref[...])


def matmul_nt(a, wt, layer, n_cols, *, row0=0, tm=1024, tn=512, name="matmul_nt"):
    m, k = a.shape
    rb0 = row0 // tn
    return pl.pallas_call(
        _mm_nt_kernel,
        grid=(n_cols // tn, m // tm),
        in_specs=[pl.BlockSpec((tm, k), lambda j, i: (i, 0)),
                  pl.BlockSpec((None, tn, k), lambda j, i: (layer, rb0 + j, 0))],
        out_specs=pl.BlockSpec((tm, tn), lambda j, i: (i, j)),
        out_shape=jax.ShapeDtypeStruct((m, n_cols), F32),
        scratch_shapes=[pltpu.VMEM((tn, k), BF16)],
        compiler_params=_cparams(("parallel", "arbitrary")),
        name=name,
    )(a, wt)


def _gate_proj_kernel(a_ref, wl_ref, wr_ref, o_ref, wb_ref, *, shift):
    @pl.when(pl.program_id(1) == 0)
    def _():
        tn = wb_ref.shape[0]
        wb_ref[0:tn - shift, :] = wl_ref[shift:tn, :].astype(BF16)
        wb_ref[tn - shift:tn, :] = wr_ref[0:shift, :].astype(BF16)

    o_ref[...] = _dot_nt(a_ref[...], wb_ref[...])


def gate_proj(a, wt, layer, tm=512, tn=512):
    m, k = a.shape
    rb0, shift = IN_GATES // tn, IN_GATES % tn
    assert shift % 16 == 0 and (tn - shift) % 16 == 0
    return pl.pallas_call(
        functools.partial(_gate_proj_kernel, shift=shift),
        grid=(2 * D_MODEL // tn, m // tm),
        in_specs=[pl.BlockSpec((tm, k), lambda j, i: (i, 0)),
                  pl.BlockSpec((None, tn, k), lambda j, i: (layer, rb0 + j, 0)),
                  pl.BlockSpec((None, tn, k), lambda j, i: (layer, rb0 + j + 1, 0))],
        out_specs=pl.BlockSpec((tm, tn), lambda j, i: (i, j)),
        out_shape=jax.ShapeDtypeStruct((m, 2 * D_MODEL), F32),
        scratch_shapes=[pltpu.VMEM((tn, k), BF16)],
        compiler_params=_cparams(("parallel", "arbitrary")),
        name="gate_proj",
    )(a, wt, wt)


def _mmk_res_kernel(a_ref, b_ref, r_ref, o_ref, acc_ref, *, k_rows):
    kk = pl.program_id(2)
    tk = b_ref.shape[0]

    @pl.when(kk == 0)
    def _():
        acc_ref[...] = r_ref[...]

    row = lax.broadcasted_iota(jnp.int32, b_ref.shape, 0)
    b = jnp.where(row < k_rows - kk * tk, b_ref[...], 0.0).astype(BF16)
    acc_ref[...] += jnp.dot(a_ref[...], b, preferred_element_type=F32)

    @pl.when(kk == pl.num_programs(2) - 1)
    def _():
        o_ref[...] = acc_ref[...]


def matmul_ktiled_res(a, w, layer, res, *, row0=0, rows=None, tm=1024, tn=1024, tk=1408, name="matmul_k"):
    k = a.shape[1]
    m = a.shape[0] if rows is None else rows
    n = w.shape[2]
    rb0 = row0 // tm
    return pl.pallas_call(
        functools.partial(_mmk_res_kernel, k_rows=w.shape[1]),
        grid=(m // tm, n // tn, k // tk),
        in_specs=[
            pl.BlockSpec((tm, tk), lambda i, j, kk: (rb0 + i, kk)),
            pl.BlockSpec((None, tk, tn), lambda i, j, kk: (layer, kk, j)),
            pl.BlockSpec((tm, tn), lambda i, j, kk: (rb0 + i, j)),
        ],
        out_specs=pl.BlockSpec((tm, tn), lambda i, j, kk: (i, j)),
        out_shape=jax.ShapeDtypeStruct((m, n), F32),
        scratch_shapes=[pltpu.VMEM((tm, tn), F32)],
        compiler_params=_cparams(("parallel", "parallel", "arbitrary")),
        name=name,
    )(a, w, res)


def _merge_kernel(att_ref, rw_ref, wa_ref, wr_ref, ga_ref, gr_ref, o_ref, wab_ref, wrb_ref):
    _cast_weight_once(wa_ref, wab_ref)
    _cast_weight_once(wr_ref, wrb_ref)
    ya = jnp.dot(att_ref[...], wab_ref[...], preferred_element_type=F32)
    yr = jnp.dot(rw_ref[...], wrb_ref[...], preferred_element_type=F32)
    o_ref[...] = (jax.nn.sigmoid(ga_ref[...]) * ya + jax.nn.sigmoid(gr_ref[...]) * yr).astype(o_ref.dtype)


def merge_branches(att, rw, wa, wr, layer, gates, tm=1024, tn=512):
    n = att.shape[0]
    gr0 = D_MODEL // tn
    return pl.pallas_call(
        _merge_kernel,
        grid=(D_MODEL // tn, n // tm),
        in_specs=[
            pl.BlockSpec((tm, ATT_W), lambda j, i: (i, 0)),
            pl.BlockSpec((tm, RW), lambda j, i: (i, 0)),
            pl.BlockSpec((None, ATT_W, tn), lambda j, i: (layer, 0, j)),
            pl.BlockSpec((None, RW, tn), lambda j, i: (layer, 0, j)),
            pl.BlockSpec((tm, tn), lambda j, i: (i, j)),
            pl.BlockSpec((tm, tn), lambda j, i: (i, gr0 + j)),
        ],
        out_specs=pl.BlockSpec((tm, tn), lambda j, i: (i, j)),
        out_shape=jax.ShapeDtypeStruct((n, D_MODEL), BF16),
        scratch_shapes=[pltpu.VMEM((ATT_W, tn), BF16), pltpu.VMEM((RW, tn), BF16)],
        compiler_params=_cparams(("parallel", "arbitrary")),
        name="merge_branches",
    )(att, rw, wa, wr, gates, gates)


def _qkv_proj_kernel(a_ref, w_ref, gain_ref, cos_ref, sa_ref, sb_ref, o_ref, wb_ref):
    _cast_weight_once(w_ref, wb_ref)
    acc = _dot_nt(a_ref[...], wb_ref[...])
    j = pl.program_id(0)

    @pl.when(j < 5)
    def _():
        cos, sa, sb = cos_ref[...], sa_ref[...], sb_ref[...]
        gain = gain_ref[0]
        for h in range(4):
            x = acc[:, h * HEAD:(h + 1) * HEAD]
            ms = jnp.mean(x * x, axis=-1, keepdims=True)
            y = x * lax.rsqrt(ms + NORM_EPS) * gain
            y = y * cos + pltpu.roll(y, 96, axis=1) * sa + pltpu.roll(y, 32, axis=1) * sb
            o_ref[:, h * HEAD:(h + 1) * HEAD] = y.astype(o_ref.dtype)

    @pl.when(j == 5)
    def _():
        o_ref[...] = acc.astype(o_ref.dtype)


def _rope_tables():
    t = np.arange(S_SEQ)
    row = (t // GRID_W).astype(np.float32)
    col = (t % GRID_W).astype(np.float32)
    axis_dim = HEAD // 2
    inv = jnp.asarray(ROPE_THETA, F32) ** (-jnp.arange(0, axis_dim, 2, dtype=F32) / axis_dim)
    ang_r = jnp.asarray(row)[:, None] * inv
    ang_c = jnp.asarray(col)[:, None] * inv
    cr, sr, cc, sc = jnp.cos(ang_r), jnp.sin(ang_r), jnp.cos(ang_c), jnp.sin(ang_c)
    z = jnp.zeros_like(sr)
    cos = jnp.concatenate([cr, cr, cc, cc], axis=1)
    sin_a = jnp.concatenate([-sr, z, -sc, z], axis=1)
    sin_b = jnp.concatenate([z, sr, z, sc], axis=1)
    return cos, sin_a, sin_b


def qkv_proj(h, w_in, layer, gains, tables, tm=1024, tn=512):
    n, k = h.shape
    npb = N_PROMPT // tm

    def tab_map(j, i):
        return (jnp.where(i < npb, i % (P_SEQ // tm), (i - npb) % (S_SEQ // tm)), 0)

    tab_spec = pl.BlockSpec((tm, HEAD), tab_map)
    return pl.pallas_call(
        _qkv_proj_kernel,
        grid=(C_R // tn, n // tm),
        in_specs=[
            pl.BlockSpec((tm, k), lambda j, i: (i, 0)),
            pl.BlockSpec((None, tn, k), lambda j, i: (layer, j, 0)),
            pl.BlockSpec((1, 1, HEAD), lambda j, i: (j, 0, 0)),
            tab_spec, tab_spec, tab_spec,
        ],
        out_specs=pl.BlockSpec((tm, tn), lambda j, i: (i, j)),
        out_shape=jax.ShapeDtypeStruct((n, C_R), BF16),
        scratch_shapes=[pltpu.VMEM((tn, k), BF16)],
        compiler_params=_cparams(("parallel", "arbitrary")),
        name="qkv_proj",
    )(h, w_in, gains, *tables)


def _attn_kernel(q_ref, k_ref, v_ref, o_ref, v1_ref):
    @pl.when(pl.program_id(2) == 0)
    def _():
        v1_ref[:, 0:HEAD] = v_ref[...]
        v1_ref[:, HEAD:2 * HEAD] = jnp.ones(v_ref.shape, v1_ref.dtype)

    k = k_ref[...]
    v1 = v1_ref[...]
    for g in range(GROUPS):
        q = q_ref[:, g * HEAD:(g + 1) * HEAD]
        s = lax.dot_general(q, k, (((1,), (1,)), ((), ())), preferred_element_type=F32)
        p = jnp.exp2(s - jnp.max(s, axis=-1, keepdims=True))
        ol = jnp.dot(p.astype(BF16), v1, preferred_element_type=F32)
        o_ref[:, g * HEAD:(g + 1) * HEAD] = (ol[:, 0:HEAD] / ol[:, HEAD:2 * HEAD]).astype(o_ref.dtype)


def attention(qkv, row0, batch, seq, tq=512):
    qb0 = row0 // tq
    sb0 = row0 // seq
    nq = seq // tq
    return pl.pallas_call(
        _attn_kernel,
        grid=(batch, KV_HEADS, nq),
        in_specs=[
            pl.BlockSpec((tq, GROUPS * HEAD), lambda b, h, i: (qb0 + b * nq + i, h)),
            pl.BlockSpec((seq, HEAD), lambda b, h, i: (sb0 + b, C_K // HEAD + h)),
            pl.BlockSpec((seq, HEAD), lambda b, h, i: (sb0 + b, C_V // HEAD + h)),
        ],
        out_specs=pl.BlockSpec((tq, GROUPS * HEAD), lambda b, h, i: (b * nq + i, h)),
        out_shape=jax.ShapeDtypeStruct((batch * seq, ATT_W), BF16),
        scratch_shapes=[pltpu.VMEM((seq, 2 * HEAD), BF16)],
        compiler_params=_cparams(("parallel", "parallel", "arbitrary")),
        name="attention",
    )(qkv, qkv, qkv)


def _rwkv_prep_kernel(xr, xr_p, xr_n, xk, xk_p, xk_n, xv, xv_p, xv_n, wl_ref, al_ref, gl_ref,
                      cr, ck, cv, w0, w2, a0, a2, g2, kkw, kaw, rkw, bd_ref,
                      r_o, v_o, kk_o, bonus_o, g_o, lw_o, kd_o, b_o, *, tm):
    row0 = pl.program_id(0) * tm

    def conv(x_ref, p_ref, n_ref, c_ref):
        x = x_ref[...]
        xp, xn = _neighbours(x, p_ref[7:8, :], n_ref[0:1, :], row0)
        return xp * c_ref[0:1, :] + x * c_ref[1:2, :] + xn * c_ref[2:3, :]

    r = conv(xr, xr_p, xr_n, cr)
    k = conv(xk, xk_p, xk_n, ck)
    v = conv(xv, xv_p, xv_n, cv)
    bd = bd_ref[...]

    def segsum(z):
        return _head_sum(z, bd)

    kkr = k * kkw[...]
    kk = kkr / jnp.maximum(jnp.sqrt(segsum(kkr * kkr)), 1e-12)
    g = jnp.dot(jax.nn.sigmoid(gl_ref[...]).astype(BF16), g2[...], preferred_element_type=F32)
    r_o[...] = r
    v_o[...] = v.astype(v_o.dtype)
    kk_o[...] = kk
    g_o[...] = g
    bc = jnp.zeros_like(r)
    for d in range(2):
        wl = jnp.tanh(wl_ref[:, d * LORA:(d + 1) * LORA]).astype(BF16)
        w_raw = w0[d:d + 1, :] + jnp.dot(wl, w2[d], preferred_element_type=F32)
        lw_o[d] = jax.nn.sigmoid(w_raw) * (-float(np.exp(-0.5)))
        al = al_ref[:, d * LORA:(d + 1) * LORA].astype(BF16)
        a = jax.nn.sigmoid(a0[d:d + 1, :] + jnp.dot(al, a2[d], preferred_element_type=F32))
        kd = k * (1.0 + (a - 1.0) * kaw[...])
        kd_o[d] = kd
        b_o[d] = a * kk
        bc = bc + r * kd * rkw[...]
    bonus_o[...] = segsum(bc) * v


def _head_block_ones(width):
    i = np.arange(width) // RK
    return jnp.asarray((i[:, None] == i[None, :]).astype(np.float32)).astype(BF16)


def _head_sum(z, ones_bd):
    hi = z.astype(BF16)
    lo = (z - hi.astype(F32)).astype(BF16)
    return (jnp.dot(hi, ones_bd, preferred_element_type=F32)
            + jnp.dot(lo, ones_bd, preferred_element_type=F32))


def rwkv_prep(proj, p, tm=256, tc=512):
    n = proj.shape[0]
    nrb = n // 8
    tb = tm // 8

    def main(c0):
        return pl.BlockSpec((tm, tc), lambda i, j: (i, c0 // tc + j))

    def prev(c0):
        return pl.BlockSpec((8, tc), lambda i, j: (jnp.maximum(i * tb - 1, 0), c0 // tc + j))

    def nxt(c0):
        return pl.BlockSpec((8, tc), lambda i, j: (jnp.minimum((i + 1) * tb, nrb - 1), c0 // tc + j))

    def chan(rows, c0=0):
        return pl.BlockSpec((rows, tc), lambda i, j: (0, c0 // tc + j))

    in_specs = []
    for c0 in (C_R, C_RK, C_RV):
        in_specs += [main(c0 - C_R), prev(c0 - C_R), nxt(c0 - C_R)]
    in_specs += [
        pl.BlockSpec((tm, 2 * LORA), lambda i, j: (i, (C_WL - C_R) // (2 * LORA))),
        pl.BlockSpec((tm, 2 * LORA), lambda i, j: (i, (C_AL - C_R) // (2 * LORA))),
        pl.BlockSpec((tm, GATE_LORA_PAD), lambda i, j: (i, (C_GL - C_R) // GATE_LORA_PAD)),
        chan(3, 0), chan(3, RW), chan(3, 2 * RW),
        chan(2),
        pl.BlockSpec((2, LORA, tc), lambda i, j: (0, 0, j)),
        chan(2),
        pl.BlockSpec((2, LORA, tc), lambda i, j: (0, 0, j)),
        pl.BlockSpec((GATE_LORA_PAD, tc), lambda i, j: (0, j)),
        chan(1), chan(1), chan(1),
        pl.BlockSpec((tc, tc), lambda i, j: (0, 0)),
    ]
    one = pl.BlockSpec((tm, tc), lambda i, j: (i, j))
    two = pl.BlockSpec((2, tm, tc), lambda i, j: (0, i, j))
    s1 = jax.ShapeDtypeStruct((n, RW), F32)
    s2 = jax.ShapeDtypeStruct((2, n, RW), F32)
    return pl.pallas_call(
        functools.partial(_rwkv_prep_kernel, tm=tm),
        grid=(n // tm, RW // tc),
        in_specs=in_specs,
        out_specs=[one, one, one, one, one, two, two, two],
        out_shape=[s1, jax.ShapeDtypeStruct((n, RW), BF16), s1, s1, s1, s2, s2, s2],
        compiler_params=_cparams(("parallel", "parallel")),
        name="rwkv_prep",
    )(proj, proj, proj, proj, proj, proj, proj, proj, proj, proj, proj, proj,
      p["rwkv_conv"], p["rwkv_conv"], p["rwkv_conv"], p["decay_w0"], p["decay_w2"], p["iclr_a0"],
      p["iclr_a2"], p["gate_g2"], p["k_k"], p["k_a"], p["r_k"], _head_block_ones(tc))


def _scan_kernel(r_ref, v_ref, kk_ref, lw_ref, kd_ref, b_ref, y_ref, s_ref, *, pg, nc):
    d = pl.program_id(0)
    c = pl.program_id(2)
    chunk = c + d * (nc - 1 - 2 * c)
    npc = N_PROMPT // CHUNK
    in_p = chunk < npc
    per = jnp.where(in_p, P_SEQ // CHUNK, S_SEQ // CHUNK)
    rel = jnp.where(in_p, chunk, chunk - npc)
    @pl.when((rel + d) % per == 0)
    def _():
        s_ref[...] = jnp.zeros_like(s_ref)

    sgn = 1 - 2 * d
    C = CHUNK
    t_i = lax.broadcasted_iota(jnp.int32, (C, 2 * C), 0)
    s_i = lax.broadcasted_iota(jnp.int32, (C, 2 * C), 1) & (C - 1)
    rel_ts = (t_i - s_i) * sgn
    strict = rel_ts > 0
    incl = rel_ts >= 0
    lane = lax.broadcasted_iota(jnp.int32, (C, 2 * C), 1)
    m_a = lane < C
    tt = lax.broadcasted_iota(jnp.int32, (C, C), 0)
    ss = lax.broadcasted_iota(jnp.int32, (C, C), 1)
    tri = jnp.where((tt - ss) * sgn >= 0, 1.0, 0.0).astype(BF16)
    ri = lax.broadcasted_iota(jnp.int32, (2 * C, 2 * C), 0)
    ci = lax.broadcasted_iota(jnp.int32, (2 * C, 2 * C), 1)
    same_head = (ri >> 6) == (ci >> 6)
    eye = jnp.where(ri == ci, 1.0, 0.0).astype(F32)
    same8 = (ri >> 3) == (ci >> 3)
    lvl = [((ri >> (s + 1)) == (ci >> (s + 1))) & ((ri >> s) != (ci >> s)) for s in (3, 4, 5)]

    def split(x):
        return jnp.concatenate([jnp.where(m_a, x, 0.0), jnp.where(m_a, 0.0, x)], axis=0)

    def bf(x):
        return x.astype(BF16)

    def mm(a, b):
        return jnp.dot(a, b, preferred_element_type=F32)

    def mm_nt(a, b):
        return lax.dot_general(a, b, (((1,), (1,)), ((), ())), preferred_element_type=F32)

    def mm_tn(a, b):
        return lax.dot_general(a, b, (((0,), (0,)), ((), ())), preferred_element_type=F32)

    def cumsum(lw):
        hi = bf(lw)
        r1 = lw - hi.astype(F32)
        mid = bf(r1)
        lo = bf(r1 - mid.astype(F32))
        return mm(tri, hi) + (mm(tri, mid) + mm(tri, lo))

    P = range(pg)
    sls = [slice(p * 2 * C, (p + 1) * 2 * C) for p in P]
    r = [r_ref[:, sl] for sl in sls]
    v = [v_ref[:, sl] for sl in sls]
    kk = [kk_ref[:, sl] for sl in sls]
    lw = [lw_ref[0, :, sl] for sl in sls]
    kd = [kd_ref[0, :, sl] for sl in sls]
    b = [b_ref[0, :, sl] for sl in sls]
    cl = [cumsum(lw[p]) for p in P]
    tot = [jnp.sum(lw[p], axis=0, keepdims=True) for p in P]
    e_neg = [jnp.exp(-cl[p]) for p in P]
    e_end = [jnp.exp(tot[p] - cl[p]) for p in P]
    x = [bf(jnp.concatenate([r[p] * jnp.exp(cl[p]), kk[p] * jnp.exp(cl[p] - lw[p])], axis=0)) for p in P]
    z = [bf(jnp.concatenate([split(kd[p] * e_neg[p]), split(b[p] * e_neg[p])], axis=0)) for p in P]
    pm = [mm_nt(x[p], z[p]) for p in P]
    ld = [split(jnp.where(strict, pm[p][C:2 * C, 2 * C:4 * C], 0.0)) for p in P]
    l8 = [jnp.where(same8, ld[p], 0.0) for p in P]
    l8b = [bf(l8[p]) for p in P]
    l8_2 = [mm(l8b[p], l8b[p]) for p in P]
    l8_2b = [bf(l8_2[p]) for p in P]
    l8_4 = [mm(l8_2b[p], l8_2b[p]) for p in P]
    t1 = [mm(bf(eye - l8[p]), bf(eye + l8_2[p])) for p in P]
    inv = [mm(bf(t1[p]), bf(eye + l8_4[p])) for p in P]
    for msk in lvl:
        invb = [bf(inv[p]) for p in P]
        t2 = [mm(invb[p], bf(jnp.where(msk, ld[p], 0.0))) for p in P]
        inv = [inv[p] - mm(bf(t2[p]), invb[p]) for p in P]
    invb = [bf(inv[p]) for p in P]
    st = [s_ref[p] for p in P]
    xs = [mm_nt(x[p], bf(st[p])) for p in P]
    vs = [bf(split(v[p])) for p in P]
    rhs = [xs[p][C:2 * C] + mm(bf(jnp.where(strict, pm[p][C:2 * C, 0:2 * C], 0.0)), vs[p]) for p in P]
    ud = [mm(invb[p], bf(split(rhs[p]))) for p in P]
    for p in P:
        coef = jnp.concatenate([jnp.where(incl, pm[p][0:C, 0:2 * C], 0.0),
                                jnp.where(incl, -pm[p][0:C, 2 * C:4 * C], 0.0)], axis=1)
        y_ref[0, :, sls[p]] = xs[p][0:C] + mm(bf(coef), jnp.concatenate([vs[p], bf(ud[p])], axis=0))
    for p in P:
        u = ud[p][0:C] + ud[p][C:2 * C]
        upd = mm_tn(jnp.concatenate([v[p], bf(u)], axis=0),
                    bf(jnp.concatenate([kd[p] * e_end[p], -(b[p] * e_end[p])], axis=0)))
        s_ref[p] = st[p] * jnp.exp(tot[p]) + jnp.where(same_head, upd, 0.0)


def rwkv_scan(r, v, kk, lw, kd, b, pg=16):
    n = r.shape[0]
    nc = n // CHUNK
    w = pg * 2 * CHUNK

    def cmap(d, g, c):
        return (c + d * (nc - 1 - 2 * c), g)

    def dmap(d, g, c):
        return (d, c + d * (nc - 1 - 2 * c), g)

    one = pl.BlockSpec((CHUNK, w), cmap)
    two = pl.BlockSpec((1, CHUNK, w), dmap)
    return pl.pallas_call(
        functools.partial(_scan_kernel, pg=pg, nc=nc),
        grid=(2, RW // w, nc),
        in_specs=[one, one, one, two, two, two],
        out_specs=two,
        out_shape=jax.ShapeDtypeStruct((2, n, RW), F32),
        scratch_shapes=[pltpu.VMEM((pg, 2 * CHUNK, 2 * CHUNK), F32)],
        compiler_params=_cparams(("parallel", "parallel", "arbitrary")),
        name="rwkv_scan",
    )(r, v, kk, lw, kd, b)


def _rwkv_post_kernel(y_ref, bonus_ref, g_ref, w_ref, b_ref, bd_ref, o_ref):
    bd = bd_ref[...]

    def segmean(z):
        return _head_sum(z, bd) * (1.0 / RK)

    y = y_ref[0] + y_ref[1]
    yc = y - segmean(y)
    var = segmean(yc * yc)
    out = yc * lax.rsqrt(var + LNX_EPS) * w_ref[...] + b_ref[...]
    o_ref[...] = ((out + bonus_ref[...]) * g_ref[...]).astype(o_ref.dtype)


def rwkv_post(y, bonus, g, lnx_w, lnx_b, tm=256, tc=512):
    n = bonus.shape[0]
    one = pl.BlockSpec((tm, tc), lambda i, j: (i, j))
    ch = pl.BlockSpec((1, tc), lambda i, j: (0, j))
    return pl.pallas_call(
        _rwkv_post_kernel,
        grid=(n // tm, RW // tc),
        in_specs=[pl.BlockSpec((2, tm, tc), lambda i, j: (0, i, j)), one, one, ch, ch,
                  pl.BlockSpec((tc, tc), lambda i, j: (0, 0))],
        out_specs=one,
        out_shape=jax.ShapeDtypeStruct((n, RW), BF16),
        compiler_params=_cparams(("parallel", "parallel")),
        name="rwkv_post",
    )(y, bonus, g, lnx_w, lnx_b, _head_block_ones(tc))


HALO = 16


FF_TN = 256
FF_TILES = D_FF // FF_TN


def _ffn_up_kernel(h_ref, hp_ref, hn_ref, wv_ref, wg_ref, cv, cg, bv, bg, o_ref, wvb_ref, wgb_ref, *, tm):
    j = pl.program_id(0)

    @pl.when(j < FF_TILES)
    def _():
        _cast_weight_once(wv_ref, wvb_ref)
        _cast_weight_once(wg_ref, wgb_ref)
        row0 = pl.program_id(1) * tm
        h = h_ref[...]
        edge = jnp.concatenate([hp_ref[...], hn_ref[...]], axis=0)

        def branch(wb_ref, c_ref, b_ref):
            w = wb_ref[...]
            u = jnp.dot(h, w, preferred_element_type=F32)
            ue = jnp.dot(edge, w, preferred_element_type=F32)
            up, un = _neighbours(u, ue[HALO - 1:HALO, :], ue[HALO:HALO + 1, :], row0)
            return up * c_ref[0:1, :] + u * c_ref[1:2, :] + un * c_ref[2:3, :] + b_ref[...]

        gate = branch(wgb_ref, cg, bg)
        act = gate * jax.nn.sigmoid(gate)
        o_ref[...] = (act * branch(wvb_ref, cv, bv)).astype(o_ref.dtype)

    @pl.when(j == FF_TILES)
    def _():
        o_ref[...] = jnp.zeros_like(o_ref)


def ffn_up_act(h, w_up, conv_w, conv_b, layer, tm=1024):
    n, k = h.shape
    tn = FF_TN
    nhb = n // HALO
    tb = tm // HALO

    def col(j):
        return jnp.minimum(j, FF_TILES - 1)

    def chan(rows, o):
        return pl.BlockSpec((None, rows, tn), lambda j, i: (layer, 0, o + col(j)))

    return pl.pallas_call(
        functools.partial(_ffn_up_kernel, tm=tm),
        grid=(D_FF_PAD // tn, n // tm),
        in_specs=[
            pl.BlockSpec((tm, k), lambda j, i: (i, 0)),
            pl.BlockSpec((HALO, k), lambda j, i: (jnp.maximum(i * tb - 1, 0), 0)),
            pl.BlockSpec((HALO, k), lambda j, i: (jnp.minimum((i + 1) * tb, nhb - 1), 0)),
            chan(k, 0), chan(k, FF_TILES),
            chan(3, 0), chan(3, FF_TILES), chan(1, 0), chan(1, FF_TILES),
        ],
        out_specs=pl.BlockSpec((tm, tn), lambda j, i: (i, j)),
        out_shape=jax.ShapeDtypeStruct((n, D_FF_PAD), BF16),
        scratch_shapes=[pltpu.VMEM((k, tn), BF16), pltpu.VMEM((k, tn), BF16)],
        compiler_params=_cparams(("parallel", "arbitrary")),
        name="ffn_up_act",
    )(h, h, h, w_up, w_up, conv_w, conv_w, conv_b, conv_b)


def _layer_params(l, norm_mix, q_gain, k_gain, rwkv_conv, decay_w0, decay_w2, iclr_a0, iclr_a2,
                  gate_g2, k_k, k_a, r_k, lnx_w, lnx_b, norm_ffn):
    gains = jnp.stack([q_gain[l] * (HEAD ** -0.5 * float(np.log2(np.e)))] * 4 + [k_gain[l]] * 2)[:, None, :]
    return dict(
        norm_mix=norm_mix[l], gains=gains, rwkv_conv=rwkv_conv[l],
        decay_w0=decay_w0[l], decay_w2=decay_w2[l].astype(BF16),
        iclr_a0=iclr_a0[l], iclr_a2=iclr_a2[l].astype(BF16),
        gate_g2=jnp.pad(gate_g2[l], ((0, GATE_LORA_PAD - GATE_LORA), (0, 0))).astype(BF16),
        k_k=k_k[l][None, :], k_a=k_a[l][None, :], r_k=r_k[l].reshape(1, RW),
        lnx_w=lnx_w[l][None, :], lnx_b=lnx_b[l][None, :], norm_ffn=norm_ffn[l],
    )


def _layer(x, l, p, big, tables, last):
    h = rmsnorm_cast(x, p["norm_mix"])
    qkv = qkv_proj(h, big["w_in_t"], l, p["gains"], tables)
    proj = matmul_nt(h, big["w_in_t"], l, C_GA - C_R, row0=C_R, name="in_proj")
    gates = gate_proj(h, big["w_in_t"], l)
    att = jnp.concatenate([attention(qkv, 0, P_BATCH, P_SEQ),
                           attention(qkv, N_PROMPT, S_BATCH, S_SEQ)], axis=0)
    r, v, kk, bonus, g, lw, kd, b = rwkv_prep(proj, p)
    y = rwkv_scan(r, v, kk, lw, kd, b)
    rw = rwkv_post(y, bonus, g, p["lnx_w"], p["lnx_b"])
    mixed = merge_branches(att, rw, big["w_up_attn"], big["w_up_rwkv"], l, gates)
    x = matmul(mixed, big["w_o"], l, D_MODEL, res=x, name="out_proj")
    h = rmsnorm_cast(x, p["norm_ffn"])
    act = ffn_up_act(h, big["w_ffn_up"], big["ffn_conv"], big["ffn_conv_b"], l)
    if not last:
        return matmul_ktiled_res(act, big["w_ffn_down"], l, x, name="ffn_down")
    return (matmul_ktiled_res(act, big["w_ffn_down"], l, x, row0=0, rows=N_PROMPT, name="ffn_down_prompt"),
            matmul_ktiled_res(act, big["w_ffn_down"], l, x, row0=N_PROMPT, rows=N_TOK - N_PROMPT,
                              name="ffn_down_sample"))


def kernel(x_prompt, x_sample, norm_mix, w_in, q_gain, k_gain, rwkv_conv, decay_w0, decay_w2, iclr_a0, iclr_a2, gate_g2, k_k, k_a, r_k, lnx_w, lnx_b, w_up_attn, w_up_rwkv, w_o, norm_ffn, w_ffn_up, ffn_conv, ffn_conv_b, w_ffn_down):
    small = (norm_mix, q_gain, k_gain, rwkv_conv, decay_w0, decay_w2, iclr_a0, iclr_a2,
             gate_g2, k_k, k_a, r_k, lnx_w, lnx_b, norm_ffn)
    big = dict(w_in_t=jnp.swapaxes(w_in, 1, 2),
               w_up_attn=w_up_attn, w_up_rwkv=w_up_rwkv, w_o=w_o,
               w_ffn_up=w_ffn_up, ffn_conv=ffn_conv, ffn_conv_b=ffn_conv_b[:, None, :], w_ffn_down=w_ffn_down)
    x = jnp.concatenate([x_prompt.reshape(N_PROMPT, D_MODEL), x_sample.reshape(-1, D_MODEL)], axis=0)
    tables = _rope_tables()
    depth = norm_mix.shape[0]
    for l in range(depth):
        x = _layer(x, l, _layer_params(l, *small), big, tables, last=l == depth - 1)
    y_prompt, y_sample = x
    return (y_prompt.reshape(P_BATCH, P_SEQ, D_MODEL), y_sample.reshape(S_BATCH, S_SEQ, D_MODEL))
```

```python
import functools

import jax
import jax.numpy as jnp
import numpy as np
from jax import lax
from jax.experimental import pallas as pl
from jax.experimental.pallas import tpu as pltpu

F32 = jnp.float32
BF16 = jnp.bfloat16

D_MODEL = 4096
P_BATCH, P_SEQ = 4, 2048
S_BATCH, S_SEQ = 2, 4096
N_PROMPT = P_BATCH * P_SEQ
N_TOK = N_PROMPT + S_BATCH * S_SEQ
GRID_W = 64
HEAD = 128
ATT_W = 2048
KV_W = 512
KV_HEADS = 4
GROUPS = 4
ROPE_THETA = 10000.0
RK = 64
RW = 2048
LORA = 128
GATE_LORA = 480
GATE_LORA_PAD = 512
D_FF = 11008
D_FF_PAD = 11264
NORM_EPS = 1e-6
LNX_EPS = RK * 1e-5

C_Q, C_K, C_V = 0, 2048, 2560
C_R, C_RK, C_RV = 3072, 5120, 7168
C_WL, C_AL, C_GL = 9216, 9472, 9728
C_GA = C_GL + GATE_LORA_PAD
IN_GATES = C_GL + GATE_LORA

CHUNK = 64
VMEM_LIMIT = 56 * 1024 * 1024


def _cparams(sem):
    return pltpu.CompilerParams(dimension_semantics=sem, vmem_limit_bytes=VMEM_LIMIT)


def _seq_edges(rows):
    pos = rows & jnp.where(rows < N_PROMPT, P_SEQ - 1, S_SEQ - 1)
    last = jnp.where(rows < N_PROMPT, P_SEQ - 1, S_SEQ - 1)
    return pos == 0, pos == last


def _neighbours(x, prev_row, next_row, row0):
    tm = x.shape[0]
    assert P_SEQ % tm == 0 and S_SEQ % tm == 0
    li = lax.broadcasted_iota(jnp.int32, x.shape, 0)
    starts, _ = _seq_edges(row0)
    _, ends = _seq_edges(row0 + (tm - 1))
    prev_row = jnp.where(starts, 0.0, prev_row)
    next_row = jnp.where(ends, 0.0, next_row)
    xp = jnp.where(li == 0, prev_row, pltpu.roll(x, 1, axis=0))
    xn = jnp.where(li == tm - 1, next_row, pltpu.roll(x, tm - 1, axis=0))
    return xp, xn


def _rmsnorm_kernel(x_ref, g_ref, o_ref):
    x = x_ref[...]
    ms = jnp.mean(x * x, axis=-1, keepdims=True)
    o_ref[...] = (x * lax.rsqrt(ms + NORM_EPS) * g_ref[...]).astype(o_ref.dtype)


def rmsnorm_cast(x, g, tm=512):
    n, d = x.shape
    return pl.pallas_call(
        _rmsnorm_kernel,
        grid=(n // tm,),
        in_specs=[pl.BlockSpec((tm, d), lambda i: (i, 0)), pl.BlockSpec((1, d), lambda i: (0, 0))],
        out_specs=pl.BlockSpec((tm, d), lambda i: (i, 0)),
        out_shape=jax.ShapeDtypeStruct((n, d), BF16),
        compiler_params=_cparams(("parallel",)),
        name="rmsnorm_cast",
    )(x, g.reshape(1, d))


def _cast_weight_once(w_ref, wb_ref):
    @pl.when(pl.program_id(1) == 0)
    def _():
        wb_ref[...] = w_ref[...].astype(BF16)


def _mm_kernel(a_ref, b_ref, o_ref, bb_ref):
    _cast_weight_once(b_ref, bb_ref)
    o_ref[...] = jnp.dot(a_ref[...], bb_ref[...], preferred_element_type=F32).astype(o_ref.dtype)


def _mm_res_kernel(a_ref, b_ref, r_ref, o_ref, bb_ref):
    _cast_weight_once(b_ref, bb_ref)
    o_ref[...] = r_ref[...] + jnp.dot(a_ref[...], bb_ref[...], preferred_element_type=F32)


def matmul(a, w, layer, n_cols, res=None, *, col0=0, tm=1024, tn=512, name="matmul"):
    m, k = a.shape
    cb0 = col0 // tn
    in_specs = [pl.BlockSpec((tm, k), lambda j, i: (i, 0)),
                pl.BlockSpec((None, k, tn), lambda j, i: (layer, 0, cb0 + j))]
    args = [a, w]
    kern = _mm_kernel
    if res is not None:
        in_specs.append(pl.BlockSpec((tm, tn), lambda j, i: (i, j)))
        args.append(res)
        kern = _mm_res_kernel
    return pl.pallas_call(
        kern,
        grid=(n_cols // tn, m // tm),
        in_specs=in_specs,
        out_specs=pl.BlockSpec((tm, tn), lambda j, i: (i, j)),
        out_shape=jax.ShapeDtypeStruct((m, n_cols), F32),
        scratch_shapes=[pltpu.VMEM((k, tn), BF16)],
        compiler_params=_cparams(("parallel", "arbitrary")),
        name=name,
    )(*args)


def _dot_nt(a, wt):
    return lax.dot_general(a, wt, (((1,), (1,)), ((), ())), preferred_element_type=F32)


def _mm_nt_kernel(a_ref, w_ref, o_ref, wb_ref):
    _cast_weight_once(w_ref, wb_ref)
    o_ref[...] = _dot_nt(a_ref[...], wb_ref[...])


def matmul_nt(a, wt, layer, n_cols, *, row0=0, tm=1024, tn=512, name="matmul_nt"):
    m, k = a.shape
    rb0 = row0 // tn
    return pl.pallas_call(
        _mm_nt_kernel,
        grid=(n_cols // tn, m // tm),
        in_specs=[pl.BlockSpec((tm, k), lambda j, i: (i, 0)),
                  pl.BlockSpec((None, tn, k), lambda j, i: (layer, rb0 + j, 0))],
        out_specs=pl.BlockSpec((tm, tn), lambda j, i: (i, j)),
        out_shape=jax.ShapeDtypeStruct((m, n_cols), F32),
        scratch_shapes=[pltpu.VMEM((tn, k), BF16)],
        compiler_params=_cparams(("parallel", "arbitrary")),
        name=name,
    )(a, wt)


def _gate_proj_kernel(a_ref, wl_ref, wr_ref, o_ref, wb_ref, *, shift):
    @pl.when(pl.program_id(1) == 0)
    def _():
        tn = wb_ref.shape[0]
        wb_ref[0:tn - shift, :] = wl_ref[shift:tn, :].astype(BF16)
        wb_ref[tn - shift:tn, :] = wr_ref[0:shift, :].astype(BF16)

    o_ref[...] = _dot_nt(a_ref[...], wb_ref[...])


def gate_proj(a, wt, layer, tm=512, tn=512):
    m, k = a.shape
    rb0, shift = IN_GATES // tn, IN_GATES % tn
    assert shift % 16 == 0 and (tn - shift) % 16 == 0
    return pl.pallas_call(
        functools.partial(_gate_proj_kernel, shift=shift),
        grid=(2 * D_MODEL // tn, m // tm),
        in_specs=[pl.BlockSpec((tm, k), lambda j, i: (i, 0)),
                  pl.BlockSpec((None, tn, k), lambda j, i: (layer, rb0 + j, 0)),
                  pl.BlockSpec((None, tn, k), lambda j, i: (layer, rb0 + j + 1, 0))],
        out_specs=pl.BlockSpec((tm, tn), lambda j, i: (i, j)),
        out_shape=jax.ShapeDtypeStruct((m, 2 * D_MODEL), F32),
        scratch_shapes=[pltpu.VMEM((tn, k), BF16)],
        compiler_params=_cparams(("parallel", "arbitrary")),
        name="gate_proj",
    )(a, wt, wt)


def _mmk_res_kernel(a_ref, b_ref, r_ref, o_ref, acc_ref, *, k_rows):
    kk = pl.program_id(2)
    tk = b_ref.shape[0]

    @pl.when(kk == 0)
    def _():
        acc_ref[...] = r_ref[...]

    row = lax.broadcasted_iota(jnp.int32, b_ref.shape, 0)
    b = jnp.where(row < k_rows - kk * tk, b_ref[...], 0.0).astype(BF16)
    acc_ref[...] += jnp.dot(a_ref[...], b, preferred_element_type=F32)

    @pl.when(kk == pl.num_programs(2) - 1)
    def _():
        o_ref[...] = acc_ref[...]


def matmul_ktiled_res(a, w, layer, res, *, row0=0, rows=None, tm=1024, tn=1024, tk=1408, name="matmul_k"):
    k = a.shape[1]
    m = a.shape[0] if rows is None else rows
    n = w.shape[2]
    rb0 = row0 // tm
    return pl.pallas_call(
        functools.partial(_mmk_res_kernel, k_rows=w.shape[1]),
        grid=(m // tm, n // tn, k // tk),
        in_specs=[
            pl.BlockSpec((tm, tk), lambda i, j, kk: (rb0 + i, kk)),
            pl.BlockSpec((None, tk, tn), lambda i, j, kk: (layer, kk, j)),
            pl.BlockSpec((tm, tn), lambda i, j, kk: (rb0 + i, j)),
        ],
        out_specs=pl.BlockSpec((tm, tn), lambda i, j, kk: (i, j)),
        out_shape=jax.ShapeDtypeStruct((m, n), F32),
        scratch_shapes=[pltpu.VMEM((tm, tn), F32)],
        compiler_params=_cparams(("parallel", "parallel", "arbitrary")),
        name=name,
    )(a, w, res)


def _merge_kernel(att_ref, rw_ref, wa_ref, wr_ref, ga_ref, gr_ref, o_ref, wab_ref, wrb_ref):
    _cast_weight_once(wa_ref, wab_ref)
    _cast_weight_once(wr_ref, wrb_ref)
    ya = jnp.dot(att_ref[...], wab_ref[...], preferred_element_type=F32)
    yr = jnp.dot(rw_ref[...], wrb_ref[...], preferred_element_type=F32)
    o_ref[...] = (jax.nn.sigmoid(ga_ref[...]) * ya + jax.nn.sigmoid(gr_ref[...]) * yr).astype(o_ref.dtype)


def merge_branches(att, rw, wa, wr, layer, gates, tm=1024, tn=512):
    n = att.shape[0]
    gr0 = D_MODEL // tn
    return pl.pallas_call(
        _merge_kernel,
        grid=(D_MODEL // tn, n // tm),
        in_specs=[
            pl.BlockSpec((tm, ATT_W), lambda j, i: (i, 0)),
            pl.BlockSpec((tm, RW), lambda j, i: (i, 0)),
            pl.BlockSpec((None, ATT_W, tn), lambda j, i: (layer, 0, j)),
            pl.BlockSpec((None, RW, tn), lambda j, i: (layer, 0, j)),
            pl.BlockSpec((tm, tn), lambda j, i: (i, j)),
            pl.BlockSpec((tm, tn), lambda j, i: (i, gr0 + j)),
        ],
        out_specs=pl.BlockSpec((tm, tn), lambda j, i: (i, j)),
        out_shape=jax.ShapeDtypeStruct((n, D_MODEL), BF16),
        scratch_shapes=[pltpu.VMEM((ATT_W, tn), BF16), pltpu.VMEM((RW, tn), BF16)],
        compiler_params=_cparams(("parallel", "arbitrary")),
        name="merge_branches",
    )(att, rw, wa, wr, gates, gates)


def _qkv_proj_kernel(a_ref, w_ref, gain_ref, cos_ref, sa_ref, sb_ref, o_ref, wb_ref):
    _cast_weight_once(w_ref, wb_ref)
    acc = _dot_nt(a_ref[...], wb_ref[...])
    j = pl.program_id(0)

    @pl.when(j < 5)
    def _():
        cos, sa, sb = cos_ref[...], sa_ref[...], sb_ref[...]
        gain = gain_ref[0]
        for h in range(4):
            x = acc[:, h * HEAD:(h + 1) * HEAD]
            ms = jnp.mean(x * x, axis=-1, keepdims=True)
            y = x * lax.rsqrt(ms + NORM_EPS) * gain
            y = y * cos + pltpu.roll(y, 96, axis=1) * sa + pltpu.roll(y, 32, axis=1) * sb
            o_ref[:, h * HEAD:(h + 1) * HEAD] = y.astype(o_ref.dtype)

    @pl.when(j == 5)
    def _():
        o_ref[...] = acc.astype(o_ref.dtype)


def _rope_tables():
    t = np.arange(S_SEQ)
    row = (t // GRID_W).astype(np.float32)
    col = (t % GRID_W).astype(np.float32)
    axis_dim = HEAD // 2
    inv = jnp.asarray(ROPE_THETA, F32) ** (-jnp.arange(0, axis_dim, 2, dtype=F32) / axis_dim)
    ang_r = jnp.asarray(row)[:, None] * inv
    ang_c = jnp.asarray(col)[:, None] * inv
    cr, sr, cc, sc = jnp.cos(ang_r), jnp.sin(ang_r), jnp.cos(ang_c), jnp.sin(ang_c)
    z = jnp.zeros_like(sr)
    cos = jnp.concatenate([cr, cr, cc, cc], axis=1)
    sin_a = jnp.concatenate([-sr, z, -sc, z], axis=1)
    sin_b = jnp.concatenate([z, sr, z, sc], axis=1)
    return cos, sin_a, sin_b


def qkv_proj(h, w_in, layer, gains, tables, tm=1024, tn=512):
    n, k = h.shape
    npb = N_PROMPT // tm

    def tab_map(j, i):
        return (jnp.where(i < npb, i % (P_SEQ // tm), (i - npb) % (S_SEQ // tm)), 0)

    tab_spec = pl.BlockSpec((tm, HEAD), tab_map)
    return pl.pallas_call(
        _qkv_proj_kernel,
        grid=(C_R // tn, n // tm),
        in_specs=[
            pl.BlockSpec((tm, k), lambda j, i: (i, 0)),
            pl.BlockSpec((None, tn, k), lambda j, i: (layer, j, 0)),
            pl.BlockSpec((1, 1, HEAD), lambda j, i: (j, 0, 0)),
            tab_spec, tab_spec, tab_spec,
        ],
        out_specs=pl.BlockSpec((tm, tn), lambda j, i: (i, j)),
        out_shape=jax.ShapeDtypeStruct((n, C_R), BF16),
        scratch_shapes=[pltpu.VMEM((tn, k), BF16)],
        compiler_params=_cparams(("parallel", "arbitrary")),
        name="qkv_proj",
    )(h, w_in, gains, *tables)


def _attn_kernel(q_ref, k_ref, v_ref, o_ref, v1_ref):
    @pl.when(pl.program_id(2) == 0)
    def _():
        v1_ref[:, 0:HEAD] = v_ref[...]
        v1_ref[:, HEAD:2 * HEAD] = jnp.ones(v_ref.shape, v1_ref.dtype)

    k = k_ref[...]
    v1 = v1_ref[...]
    for g in range(GROUPS):
        q = q_ref[:, g * HEAD:(g + 1) * HEAD]
        s = lax.dot_general(q, k, (((1,), (1,)), ((), ())), preferred_element_type=F32)
        p = jnp.exp2(s - jnp.max(s, axis=-1, keepdims=True))
        ol = jnp.dot(p.astype(BF16), v1, preferred_element_type=F32)
        o_ref[:, g * HEAD:(g + 1) * HEAD] = (ol[:, 0:HEAD] / ol[:, HEAD:2 * HEAD]).astype(o_ref.dtype)


def attention(qkv, row0, batch, seq, tq=512):
    qb0 = row0 // tq
    sb0 = row0 // seq
    nq = seq // tq
    return pl.pallas_call(
        _attn_kernel,
        grid=(batch, KV_HEADS, nq),
        in_specs=[
            pl.BlockSpec((tq, GROUPS * HEAD), lambda b, h, i: (qb0 + b * nq + i, h)),
            pl.BlockSpec((seq, HEAD), lambda b, h, i: (sb0 + b, C_K // HEAD + h)),
            pl.BlockSpec((seq, HEAD), lambda b, h, i: (sb0 + b, C_V // HEAD + h)),
        ],
        out_specs=pl.BlockSpec((tq, GROUPS * HEAD), lambda b, h, i: (b * nq + i, h)),
        out_shape=jax.ShapeDtypeStruct((batch * seq, ATT_W), BF16),
        scratch_shapes=[pltpu.VMEM((seq, 2 * HEAD), BF16)],
        compiler_params=_cparams(("parallel", "parallel", "arbitrary")),
        name="attention",
    )(qkv, qkv, qkv)


def _rwkv_prep_kernel(xr, xr_p, xr_n, xk, xk_p, xk_n, xv, xv_p, xv_n, wl_ref, al_ref, gl_ref,
                      cr, ck, cv, w0, w2, a0, a2, g2, kkw, kaw, rkw, bd_ref,
                      r_o, v_o, kk_o, bonus_o, g_o, lw_o, kd_o, b_o, *, tm):
    row0 = pl.program_id(0) * tm

    def conv(x_ref, p_ref, n_ref, c_ref):
        x = x_ref[...]
        xp, xn = _neighbours(x, p_ref[7:8, :], n_ref[0:1, :], row0)
        return xp * c_ref[0:1, :] + x * c_ref[1:2, :] + xn * c_ref[2:3, :]

    r = conv(xr, xr_p, xr_n, cr)
    k = conv(xk, xk_p, xk_n, ck)
    v = conv(xv, xv_p, xv_n, cv)
    bd = bd_ref[...]

    def segsum(z):
        return _head_sum(z, bd)

    kkr = k * kkw[...]
    kk = kkr / jnp.maximum(jnp.sqrt(segsum(kkr * kkr)), 1e-12)
    g = jnp.dot(jax.nn.sigmoid(gl_ref[...]).astype(BF16), g2[...], preferred_element_type=F32)
    r_o[...] = r
    v_o[...] = v.astype(v_o.dtype)
    kk_o[...] = kk
    g_o[...] = g
    bc = jnp.zeros_like(r)
    for d in range(2):
        wl = jnp.tanh(wl_ref[:, d * LORA:(d + 1) * LORA]).astype(BF16)
        w_raw = w0[d:d + 1, :] + jnp.dot(wl, w2[d], preferred_element_type=F32)
        lw_o[d] = jax.nn.sigmoid(w_raw) * (-float(np.exp(-0.5)))
        al = al_ref[:, d * LORA:(d + 1) * LORA].astype(BF16)
        a = jax.nn.sigmoid(a0[d:d + 1, :] + jnp.dot(al, a2[d], preferred_element_type=F32))
        kd = k * (1.0 + (a - 1.0) * kaw[...])
        kd_o[d] = kd
        b_o[d] = a * kk
        bc = bc + r * kd * rkw[...]
    bonus_o[...] = segsum(bc) * v


def _head_block_ones(width):
    i = np.arange(width) // RK
    return jnp.asarray((i[:, None] == i[None, :]).astype(np.float32)).astype(BF16)


def _head_sum(z, ones_bd):
    hi = z.astype(BF16)
    lo = (z - hi.astype(F32)).astype(BF16)
    return (jnp.dot(hi, ones_bd, preferred_element_type=F32)
            + jnp.dot(lo, ones_bd, preferred_element_type=F32))


def rwkv_prep(proj, p, tm=256, tc=512):
    n = proj.shape[0]
    nrb = n // 8
    tb = tm // 8

    def main(c0):
        return pl.BlockSpec((tm, tc), lambda i, j: (i, c0 // tc + j))

    def prev(c0):
        return pl.BlockSpec((8, tc), lambda i, j: (jnp.maximum(i * tb - 1, 0), c0 // tc + j))

    def nxt(c0):
        return pl.BlockSpec((8, tc), lambda i, j: (jnp.minimum((i + 1) * tb, nrb - 1), c0 // tc + j))

    def chan(rows, c0=0):
        return pl.BlockSpec((rows, tc), lambda i, j: (0, c0 // tc + j))

    in_specs = []
    for c0 in (C_R, C_RK, C_RV):
        in_specs += [main(c0 - C_R), prev(c0 - C_R), nxt(c0 - C_R)]
    in_specs += [
        pl.BlockSpec((tm, 2 * LORA), lambda i, j: (i, (C_WL - C_R) // (2 * LORA))),
        pl.BlockSpec((tm, 2 * LORA), lambda i, j: (i, (C_AL - C_R) // (2 * LORA))),
        pl.BlockSpec((tm, GATE_LORA_PAD), lambda i, j: (i, (C_GL - C_R) // GATE_LORA_PAD)),
        chan(3, 0), chan(3, RW), chan(3, 2 * RW),
        chan(2),
        pl.BlockSpec((2, LORA, tc), lambda i, j: (0, 0, j)),
        chan(2),
        pl.BlockSpec((2, LORA, tc), lambda i, j: (0, 0, j)),
        pl.BlockSpec((GATE_LORA_PAD, tc), lambda i, j: (0, j)),
        chan(1), chan(1), chan(1),
        pl.BlockSpec((tc, tc), lambda i, j: (0, 0)),
    ]
    one = pl.BlockSpec((tm, tc), lambda i, j: (i, j))
    two = pl.BlockSpec((2, tm, tc), lambda i, j: (0, i, j))
    s1 = jax.ShapeDtypeStruct((n, RW), F32)
    s2 = jax.ShapeDtypeStruct((2, n, RW), F32)
    return pl.pallas_call(
        functools.partial(_rwkv_prep_kernel, tm=tm),
        grid=(n // tm, RW // tc),
        in_specs=in_specs,
        out_specs=[one, one, one, one, one, two, two, two],
        out_shape=[s1, jax.ShapeDtypeStruct((n, RW), BF16), s1, s1, s1, s2, s2, s2],
        compiler_params=_cparams(("parallel", "parallel")),
        name="rwkv_prep",
    )(proj, proj, proj, proj, proj, proj, proj, proj, proj, proj, proj, proj,
      p["rwkv_conv"], p["rwkv_conv"], p["rwkv_conv"], p["decay_w0"], p["decay_w2"], p["iclr_a0"],
      p["iclr_a2"], p["gate_g2"], p["k_k"], p["k_a"], p["r_k"], _head_block_ones(tc))


def _scan_kernel(r_ref, v_ref, kk_ref, lw_ref, kd_ref, b_ref, y_ref, s_ref, *, pg, nc):
    d = pl.program_id(0)
    c = pl.program_id(2)
    chunk = c + d * (nc - 1 - 2 * c)
    npc = N_PROMPT // CHUNK
    in_p = chunk < npc
    per = jnp.where(in_p, P_SEQ // CHUNK, S_SEQ // CHUNK)
    rel = jnp.where(in_p, chunk, chunk - npc)
    @pl.when((rel + d) % per == 0)
    def _():
        s_ref[...] = jnp.zeros_like(s_ref)

    sgn = 1 - 2 * d
    C = CHUNK
    t_i = lax.broadcasted_iota(jnp.int32, (C, 2 * C), 0)
    s_i = lax.broadcasted_iota(jnp.int32, (C, 2 * C), 1) & (C - 1)
    rel_ts = (t_i - s_i) * sgn
    strict = rel_ts > 0
    incl = rel_ts >= 0
    lane = lax.broadcasted_iota(jnp.int32, (C, 2 * C), 1)
    m_a = lane < C
    tt = lax.broadcasted_iota(jnp.int32, (C, C), 0)
    ss = lax.broadcasted_iota(jnp.int32, (C, C), 1)
    tri = jnp.where((tt - ss) * sgn >= 0, 1.0, 0.0).astype(BF16)
    ri = lax.broadcasted_iota(jnp.int32, (2 * C, 2 * C), 0)
    ci = lax.broadcasted_iota(jnp.int32, (2 * C, 2 * C), 1)
    same_head = (ri >> 6) == (ci >> 6)
    eye = jnp.where(ri == ci, 1.0, 0.0).astype(F32)
    same8 = (ri >> 3) == (ci >> 3)
    lvl = [((ri >> (s + 1)) == (ci >> (s + 1))) & ((ri >> s) != (ci >> s)) for s in (3, 4, 5)]

    def split(x):
        return jnp.concatenate([jnp.where(m_a, x, 0.0), jnp.where(m_a, 0.0, x)], axis=0)

    def bf(x):
        return x.astype(BF16)

    def mm(a, b):
        return jnp.dot(a, b, preferred_element_type=F32)

    def mm_nt(a, b):
        return lax.dot_general(a, b, (((1,), (1,)), ((), ())), preferred_element_type=F32)

    def mm_tn(a, b):
        return lax.dot_general(a, b, (((0,), (0,)), ((), ())), preferred_element_type=F32)

    def cumsum(lw):
        hi = bf(lw)
        r1 = lw - hi.astype(F32)
        mid = bf(r1)
        lo = bf(r1 - mid.astype(F32))
        return mm(tri, hi) + (mm(tri, mid) + mm(tri, lo))

    P = range(pg)
    sls = [slice(p * 2 * C, (p + 1) * 2 * C) for p in P]
    r = [r_ref[:, sl] for sl in sls]
    v = [v_ref[:, sl] for sl in sls]
    kk = [kk_ref[:, sl] for sl in sls]
    lw = [lw_ref[0, :, sl] for sl in sls]
    kd = [kd_ref[0, :, sl] for sl in sls]
    b = [b_ref[0, :, sl] for sl in sls]
    cl = [cumsum(lw[p]) for p in P]
    tot = [jnp.sum(lw[p], axis=0, keepdims=True) for p in P]
    e_neg = [jnp.exp(-cl[p]) for p in P]
    e_end = [jnp.exp(tot[p] - cl[p]) for p in P]
    x = [bf(jnp.concatenate([r[p] * jnp.exp(cl[p]), kk[p] * jnp.exp(cl[p] - lw[p])], axis=0)) for p in P]
    z = [bf(jnp.concatenate([split(kd[p] * e_neg[p]), split(b[p] * e_neg[p])], axis=0)) for p in P]
    pm = [mm_nt(x[p], z[p]) for p in P]
    ld = [split(jnp.where(strict, pm[p][C:2 * C, 2 * C:4 * C], 0.0)) for p in P]
    l8 = [jnp.where(same8, ld[p], 0.0) for p in P]
    l8b = [bf(l8[p]) for p in P]
    l8_2 = [mm(l8b[p], l8b[p]) for p in P]
    l8_2b = [bf(l8_2[p]) for p in P]
    l8_4 = [mm(l8_2b[p], l8_2b[p]) for p in P]
    t1 = [mm(bf(eye - l8[p]), bf(eye + l8_2[p])) for p in P]
    inv = [mm(bf(t1[p]), bf(eye + l8_4[p])) for p in P]
    for msk in lvl:
        invb = [bf(inv[p]) for p in P]
        t2 = [mm(invb[p], bf(jnp.where(msk, ld[p], 0.0))) for p in P]
        inv = [inv[p] - mm(bf(t2[p]), invb[p]) for p in P]
    invb = [bf(inv[p]) for p in P]
    st = [s_ref[p] for p in P]
    xs = [mm_nt(x[p], bf(st[p])) for p in P]
    vs = [bf(split(v[p])) for p in P]
    rhs = [xs[p][C:2 * C] + mm(bf(jnp.where(strict, pm[p][C:2 * C, 0:2 * C], 0.0)), vs[p]) for p in P]
    ud = [mm(invb[p], bf(split(rhs[p]))) for p in P]
    for p in P:
        coef = jnp.concatenate([jnp.where(incl, pm[p][0:C, 0:2 * C], 0.0),
                                jnp.where(incl, -pm[p][0:C, 2 * C:4 * C], 0.0)], axis=1)
        y_ref[0, :, sls[p]] = xs[p][0:C] + mm(bf(coef), jnp.concatenate([vs[p], bf(ud[p])], axis=0))
    for p in P:
        u = ud[p][0:C] + ud[p][C:2 * C]
        upd = mm_tn(jnp.concatenate([v[p], bf(u)], axis=0),
                    bf(jnp.concatenate([kd[p] * e_end[p], -(b[p] * e_end[p])], axis=0)))
        s_ref[p] = st[p] * jnp.exp(tot[p]) + jnp.where(same_head, upd, 0.0)


def rwkv_scan(r, v, kk, lw, kd, b, pg=16):
    n = r.shape[0]
    nc = n // CHUNK
    w = pg * 2 * CHUNK

    def cmap(d, g, c):
        return (c + d * (nc - 1 - 2 * c), g)

    def dmap(d, g, c):
        return (d, c + d * (nc - 1 - 2 * c), g)

    one = pl.BlockSpec((CHUNK, w), cmap)
    two = pl.BlockSpec((1, CHUNK, w), dmap)
    return pl.pallas_call(
        functools.partial(_scan_kernel, pg=pg, nc=nc),
        grid=(2, RW // w, nc),
        in_specs=[one, one, one, two, two, two],
        out_specs=two,
        out_shape=jax.ShapeDtypeStruct((2, n, RW), F32),
        scratch_shapes=[pltpu.VMEM((pg, 2 * CHUNK, 2 * CHUNK), F32)],
        compiler_params=_cparams(("parallel", "parallel", "arbitrary")),
        name="rwkv_scan",
    )(r, v, kk, lw, kd, b)


def _rwkv_post_kernel(y_ref, bonus_ref, g_ref, w_ref, b_ref, bd_ref, o_ref):
    bd = bd_ref[...]

    def segmean(z):
        return _head_sum(z, bd) * (1.0 / RK)

    y = y_ref[0] + y_ref[1]
    yc = y - segmean(y)
    var = segmean(yc * yc)
    out = yc * lax.rsqrt(var + LNX_EPS) * w_ref[...] + b_ref[...]
    o_ref[...] = ((out + bonus_ref[...]) * g_ref[...]).astype(o_ref.dtype)


def rwkv_post(y, bonus, g, lnx_w, lnx_b, tm=512, tc=512):
    n = bonus.shape[0]
    one = pl.BlockSpec((tm, tc), lambda i, j: (i, j))
    ch = pl.BlockSpec((1, tc), lambda i, j: (0, j))
    return pl.pallas_call(
        _rwkv_post_kernel,
        grid=(n // tm, RW // tc),
        in_specs=[pl.BlockSpec((2, tm, tc), lambda i, j: (0, i, j)), one, one, ch, ch,
                  pl.BlockSpec((tc, tc), lambda i, j: (0, 0))],
        out_specs=one,
        out_shape=jax.ShapeDtypeStruct((n, RW), BF16),
        compiler_params=_cparams(("parallel", "parallel")),
        name="rwkv_post",
    )(y, bonus, g, lnx_w, lnx_b, _head_block_ones(tc))


HALO = 16


FF_TN = 256
FF_TILES = D_FF // FF_TN


def _ffn_up_kernel(h_ref, hp_ref, hn_ref, wv_ref, wg_ref, cv, cg, bv, bg, o_ref, wvb_ref, wgb_ref, *, tm):
    j = pl.program_id(0)

    @pl.when(j < FF_TILES)
    def _():
        _cast_weight_once(wv_ref, wvb_ref)
        _cast_weight_once(wg_ref, wgb_ref)
        row0 = pl.program_id(1) * tm
        h = h_ref[...]
        edge = jnp.concatenate([hp_ref[...], hn_ref[...]], axis=0)

        def branch(wb_ref, c_ref, b_ref):
            w = wb_ref[...]
            u = jnp.dot(h, w, preferred_element_type=F32)
            ue = jnp.dot(edge, w, preferred_element_type=F32)
            up, un = _neighbours(u, ue[HALO - 1:HALO, :], ue[HALO:HALO + 1, :], row0)
            return up * c_ref[0:1, :] + u * c_ref[1:2, :] + un * c_ref[2:3, :] + b_ref[...]

        gate = branch(wgb_ref, cg, bg)
        act = gate * jax.nn.sigmoid(gate)
        o_ref[...] = (act * branch(wvb_ref, cv, bv)).astype(o_ref.dtype)

    @pl.when(j == FF_TILES)
    def _():
        o_ref[...] = jnp.zeros_like(o_ref)


def ffn_up_act(h, w_up, conv_w, conv_b, layer, tm=1024):
    n, k = h.shape
    tn = FF_TN
    nhb = n // HALO
    tb = tm // HALO

    def col(j):
        return jnp.minimum(j, FF_TILES - 1)

    def chan(rows, o):
        return pl.BlockSpec((None, rows, tn), lambda j, i: (layer, 0, o + col(j)))

    return pl.pallas_call(
        functools.partial(_ffn_up_kernel, tm=tm),
        grid=(D_FF_PAD // tn, n // tm),
        in_specs=[
            pl.BlockSpec((tm, k), lambda j, i: (i, 0)),
            pl.BlockSpec((HALO, k), lambda j, i: (jnp.maximum(i * tb - 1, 0), 0)),
            pl.BlockSpec((HALO, k), lambda j, i: (jnp.minimum((i + 1) * tb, nhb - 1), 0)),
            chan(k, 0), chan(k, FF_TILES),
            chan(3, 0), chan(3, FF_TILES), chan(1, 0), chan(1, FF_TILES),
        ],
        out_specs=pl.BlockSpec((tm, tn), lambda j, i: (i, j)),
        out_shape=jax.ShapeDtypeStruct((n, D_FF_PAD), BF16),
        scratch_shapes=[pltpu.VMEM((k, tn), BF16), pltpu.VMEM((k, tn), BF16)],
        compiler_params=_cparams(("parallel", "arbitrary")),
        name="ffn_up_act",
    )(h, h, h, w_up, w_up, conv_w, conv_w, conv_b, conv_b)


def _layer_params(l, norm_mix, q_gain, k_gain, rwkv_conv, decay_w0, decay_w2, iclr_a0, iclr_a2,
                  gate_g2, k_k, k_a, r_k, lnx_w, lnx_b, norm_ffn):
    gains = jnp.stack([q_gain[l] * (HEAD ** -0.5 * float(np.log2(np.e)))] * 4 + [k_gain[l]] * 2)[:, None, :]
    return dict(
        norm_mix=norm_mix[l], gains=gains, rwkv_conv=rwkv_conv[l],
        decay_w0=decay_w0[l], decay_w2=decay_w2[l].astype(BF16),
        iclr_a0=iclr_a0[l], iclr_a2=iclr_a2[l].astype(BF16),
        gate_g2=jnp.pad(gate_g2[l], ((0, GATE_LORA_PAD - GATE_LORA), (0, 0))).astype(BF16),
        k_k=k_k[l][None, :], k_a=k_a[l][None, :], r_k=r_k[l].reshape(1, RW),
        lnx_w=lnx_w[l][None, :], lnx_b=lnx_b[l][None, :], norm_ffn=norm_ffn[l],
    )


def _layer(x, l, p, big, tables, last):
    h = rmsnorm_cast(x, p["norm_mix"])
    qkv = qkv_proj(h, big["w_in_t"], l, p["gains"], tables)
    proj = matmul_nt(h, big["w_in_t"], l, C_GA - C_R, row0=C_R, name="in_proj")
    gates = gate_proj(h, big["w_in_t"], l)
    att = jnp.concatenate([attention(qkv, 0, P_BATCH, P_SEQ),
                           attention(qkv, N_PROMPT, S_BATCH, S_SEQ)], axis=0)
    r, v, kk, bonus, g, lw, kd, b = rwkv_prep(proj, p)
    y = rwkv_scan(r, v, kk, lw, kd, b)
    rw = rwkv_post(y, bonus, g, p["lnx_w"], p["lnx_b"])
    mixed = merge_branches(att, rw, big["w_up_attn"], big["w_up_rwkv"], l, gates)
    x = matmul(mixed, big["w_o"], l, D_MODEL, res=x, name="out_proj")
    h = rmsnorm_cast(x, p["norm_ffn"])
    act = ffn_up_act(h, big["w_ffn_up"], big["ffn_conv"], big["ffn_conv_b"], l)
    if not last:
        return matmul_ktiled_res(act, big["w_ffn_down"], l, x, name="ffn_down")
    return (matmul_ktiled_res(act, big["w_ffn_down"], l, x, row0=0, rows=N_PROMPT, name="ffn_down_prompt"),
            matmul_ktiled_res(act, big["w_ffn_down"], l, x, row0=N_PROMPT, rows=N_TOK - N_PROMPT,
                              name="ffn_down_sample"))


def kernel(x_prompt, x_sample, norm_mix, w_in, q_gain, k_gain, rwkv_conv, decay_w0, decay_w2, iclr_a0, iclr_a2, gate_g2, k_k, k_a, r_k, lnx_w, lnx_b, w_up_attn, w_up_rwkv, w_o, norm_ffn, w_ffn_up, ffn_conv, ffn_conv_b, w_ffn_down):
    small = (norm_mix, q_gain, k_gain, rwkv_conv, decay_w0, decay_w2, iclr_a0, iclr_a2,
             gate_g2, k_k, k_a, r_k, lnx_w, lnx_b, norm_ffn)
    big = dict(w_in_t=jnp.swapaxes(w_in, 1, 2),
               w_up_attn=w_up_attn, w_up_rwkv=w_up_rwkv, w_o=w_o,
               w_ffn_up=w_ffn_up, ffn_conv=ffn_conv, ffn_conv_b=ffn_conv_b[:, None, :], w_ffn_down=w_ffn_down)
    x = jnp.concatenate([x_prompt.reshape(N_PROMPT, D_MODEL), x_sample.reshape(-1, D_MODEL)], axis=0)
    tables = _rope_tables()
    depth = norm_mix.shape[0]
    for l in range(depth):
        x = _layer(x, l, _layer_params(l, *small), big, tables, last=l == depth - 1)
    y_prompt, y_sample = x
    return (y_prompt.reshape(P_BATCH, P_SEQ, D_MODEL), y_sample.reshape(S_BATCH, S_SEQ, D_MODEL))
```
